```python
import jax
import jax.numpy as jnp
from jax import lax
import numpy as np

D_MODEL = 1024
BATCH = 4
SEQ = 4096
DEPTH = 4

GRID_W = 64
CTX_LEN = 256
N_BRANCHES = 3
HG_HEADS = 4
HG_DK = 128
HG_DV = 128
HG_KDIM = HG_HEADS * HG_DK
HG_WIDTH = HG_HEADS * HG_DV
HG_CHUNK = 64
ATT_HEADS = 8
ATT_KV_HEADS = 2
HEAD_DIM = 64
ATT_GROUP = ATT_HEADS // ATT_KV_HEADS
ATT_WIDTH = ATT_HEADS * HEAD_DIM
ATT_KV_WIDTH = ATT_KV_HEADS * HEAD_DIM
ROPE_THETA = 10000.0
ROPE_PAIRS_AXIS = HEAD_DIM // 4
Q_BLOCK = 128
LRU_WIDTH = 512
LRU_BLOCKS = 8
LRU_BLOCK_W = LRU_WIDTH // LRU_BLOCKS
CONV_W = 4
RG_C = 8.0
D_FF = 2816
N_EXPERTS = 8
TOP_K = 2
EPS = 1e-6
IN_SIZES = (HG_KDIM, HG_KDIM, HG_KDIM, HG_WIDTH, HG_WIDTH,
            ATT_WIDTH, ATT_KV_WIDTH, ATT_KV_WIDTH,
            LRU_WIDTH, LRU_WIDTH,
            N_BRANCHES * D_MODEL)
IN_DIM = sum(IN_SIZES)
F32 = jnp.float32

kernel_name = 'hybrid_hgrn2_gqa_rglru_moe_dit'


def rms_norm(x, w):
    xf = x.astype(F32)
    y = xf * lax.rsqrt(jnp.mean(xf * xf, axis=-1, keepdims=True) + EPS)
    return (y * w.astype(F32)).astype(x.dtype)


def split_cols(u, sizes):
    idx = [int(v) for v in np.cumsum(sizes)[:-1]]
    return jnp.split(u, idx, axis=-1)


def per_token(v_ctx, v_lat, n_ctx, n_lat):
    b, d = v_lat.shape
    lat = jnp.broadcast_to(v_lat[:, None, :], (b, n_lat, d))
    if n_ctx == 0:
        return lat
    ctx = jnp.broadcast_to(v_ctx[None, None, :], (b, n_ctx, d)).astype(lat.dtype)
    return jnp.concatenate([ctx, lat], axis=1)


def flip_t(*arrs):
    return tuple(a[:, ::-1] for a in arrs)


def hgrn2_lower_bounds(lb_logits):
    p = jax.nn.softmax(lb_logits.astype(F32), axis=0)
    cum = jnp.cumsum(p, axis=0)
    return cum - cum[:1]


def hgrn2_chunked(q, k, v, logf, s0):
    b, t, h, _ = q.shape
    dv = v.shape[-1]
    n = t // HG_CHUNK

    def to_chunks(a):
        return a.reshape(b, n, HG_CHUNK, h, a.shape[-1]).transpose(1, 0, 3, 2, 4)

    lower = jnp.tril(jnp.ones((HG_CHUNK, HG_CHUNK), dtype=bool))[:, :, None]

    def step(s, blk):
        qc, kc, vc, gc = blk
        cum = jnp.cumsum(gc, axis=2)
        last = cum[:, :, -1:, :]
        o_inter = jnp.einsum('bhtk,bhkv->bhtv', qc * jnp.exp(cum), s)
        rel = jnp.exp(jnp.where(lower, cum[:, :, :, None, :] - cum[:, :, None, :, :], -jnp.inf))
        scores = jnp.einsum('bhtk,bhsk,bhtsk->bhts', qc, kc, rel)
        o_intra = jnp.einsum('bhts,bhsv->bhtv', scores, vc)
        s_new = (jnp.exp(last[:, :, 0, :])[..., None] * s
                 + jnp.einsum('bhsk,bhsv->bhkv', kc * jnp.exp(last - cum), vc))
        return s_new, o_inter + o_intra

    s_fin, o = lax.scan(step, s0, tuple(to_chunks(a) for a in (q, k, v, logf)))
    return o.transpose(1, 0, 3, 2, 4).reshape(b, t, h, dv), s_fin


def hgrn2_mixer(q, f_fwd, f_bwd, v, g, n_ctx, lb_fwd, lb_bwd, norm_w):
    b, t, _ = q.shape

    def heads(a, d):
        return a.reshape(b, t, HG_HEADS, d)

    qf = heads(jax.nn.silu(q.astype(F32)), HG_DK)
    vf = heads(v.astype(F32), HG_DV)
    s0 = jnp.zeros((b, HG_HEADS, HG_DK, HG_DV), F32)

    def direction(f_logit, lb, reverse):
        logf = jnp.logaddexp(jnp.log(lb), jnp.log1p(-lb) + jax.nn.log_sigmoid(f_logit.astype(F32)))
        logf = heads(logf, HG_DK)
        kf = -jnp.expm1(logf)
        ctx_in = tuple(a[:, :n_ctx] for a in (qf, kf, vf, logf))
        lat_in = tuple(a[:, n_ctx:] for a in (qf, kf, vf, logf))
        if reverse:
            ctx_in, lat_in = flip_t(*ctx_in), flip_t(*lat_in)
        o_ctx, s_ctx = hgrn2_chunked(*ctx_in, s0)
        o_lat, _ = hgrn2_chunked(*lat_in, s_ctx)
        if reverse:
            o_ctx, o_lat = flip_t(o_ctx, o_lat)
        return jnp.concatenate([o_ctx, o_lat], axis=1)

    o = direction(f_fwd, lb_fwd, False) + direction(f_bwd, lb_bwd, True)
    o = rms_norm(o, norm_w.reshape(HG_HEADS, HG_DV)).reshape(b, t, HG_WIDTH)
    return (o * jax.nn.silu(g.astype(F32))).astype(q.dtype)


def axial_rope_tables(rows):
    row = jnp.repeat(jnp.arange(rows, dtype=F32), GRID_W)
    col = jnp.tile(jnp.arange(GRID_W, dtype=F32), rows)
    freqs = ROPE_THETA ** (-jnp.arange(ROPE_PAIRS_AXIS, dtype=F32) / ROPE_PAIRS_AXIS)
    ang = jnp.concatenate([row[:, None] * freqs, col[:, None] * freqs], axis=-1)
    return jnp.cos(ang), jnp.sin(ang)


def apply_rope(x, cos, sin):
    b, t, h, d = x.shape
    xp = x.astype(F32).reshape(b, t, h, d // 2, 2)
    x1, x2 = xp[..., 0], xp[..., 1]
    c, s = cos[None, :, None, :], sin[None, :, None, :]
    out = jnp.stack([x1 * c - x2 * s, x1 * s + x2 * c], axis=-1)
    return out.reshape(b, t, h, d).astype(x.dtype)


def attend(q, k, v):
    b, tq = q.shape[0], q.shape[1]
    qg = q.reshape(b, tq, ATT_KV_HEADS, ATT_GROUP, HEAD_DIM)
    s = jnp.einsum('bqkgd,bskd->bkgqs', qg, k).astype(F32) * (HEAD_DIM ** -0.5)
    p = jax.nn.softmax(s, axis=-1).astype(v.dtype)
    return jnp.einsum('bkgqs,bskd->bqkgd', p, v).reshape(b, tq, ATT_HEADS, HEAD_DIM)


def gqa_mixer(q, k, v, n_ctx, rope_cos, rope_sin, q_norm_w, k_norm_w, need_ctx):
    b, t, _ = q.shape
    n_lat = t - n_ctx
    q = rms_norm(q.reshape(b, t, ATT_HEADS, HEAD_DIM), q_norm_w)
    k = rms_norm(k.reshape(b, t, ATT_KV_HEADS, HEAD_DIM), k_norm_w)
    v = v.reshape(b, t, ATT_KV_HEADS, HEAD_DIM)
    q_lat = apply_rope(q[:, n_ctx:], rope_cos, rope_sin)
    k_lat = apply_rope(k[:, n_ctx:], rope_cos, rope_sin)
    k_all = jnp.concatenate([k[:, :n_ctx], k_lat], axis=1)
    n_blk = n_lat // Q_BLOCK
    q_blocks = q_lat.reshape(b, n_blk, Q_BLOCK, ATT_HEADS, HEAD_DIM).swapaxes(0, 1)
    o_lat = lax.map(lambda qb: attend(qb, k_all, v), q_blocks)
    o_lat = o_lat.swapaxes(0, 1).reshape(b, n_lat, ATT_WIDTH)
    if not need_ctx:
        return o_lat
    o_ctx = attend(q[:, :n_ctx], k[:, :n_ctx], v[:, :n_ctx]).reshape(b, n_ctx, ATT_WIDTH)
    return jnp.concatenate([o_ctx, o_lat], axis=1)


def centred_depthwise_conv(x, w, bias):
    t = x.shape[1]
    xp = jnp.pad(x, ((0, 0), (CONV_W // 2, CONV_W - 1 - CONV_W // 2), (0, 0)))
    out = bias
    for j in range(CONV_W):
        out = out + xp[:, j:j + t] * w[j]
    return out


def linear_scan(log_a, u, h0):
    def combine(l, r):
        return l[0] * r[0], r[0] * l[1] + r[1]
    a_cum, h = lax.associative_scan(combine, (jnp.exp(log_a), u), axis=1)
    h = h + a_cum * h0[:, None, :]
    return h, h[:, -1]


def rglru_mixer(x, gate, n_ctx, conv_w, conv_b, wa, ba, wx, bx, lam):
    b = x.shape[0]
    xf = x.astype(F32)
    cw, cb = conv_w.astype(F32), conv_b.astype(F32)
    x_ctx = centred_depthwise_conv(xf[:, :n_ctx], cw, cb)
    x_lat = centred_depthwise_conv(xf[:, n_ctx:], cw, cb)
    h0 = jnp.zeros((b, LRU_WIDTH), F32)

    def block_diag(z, w):
        zb = z.reshape(z.shape[0], z.shape[1], LRU_BLOCKS, LRU_BLOCK_W)
        return jnp.einsum('btnc,ncd->btnd', zb, w.astype(F32)).reshape(z.shape)

    def coeffs(z, d):
        r = jax.nn.sigmoid(block_diag(z, wa[d]) + ba[d].astype(F32))
        i = jax.nn.sigmoid(block_diag(z, wx[d]) + bx[d].astype(F32))
        log_a = -RG_C * r * jax.nn.softplus(-lam[d].astype(F32))
        return log_a, jnp.sqrt(-jnp.expm1(2.0 * log_a)) * (i * z)

    def direction(d, reverse):
        zc, zl = flip_t(x_ctx, x_lat) if reverse else (x_ctx, x_lat)
        h_ctx, h_last = linear_scan(*coeffs(zc, d), h0)
        h_lat, _ = linear_scan(*coeffs(zl, d), h_last)
        if reverse:
            h_ctx, h_lat = flip_t(h_ctx, h_lat)
        return jnp.concatenate([h_ctx, h_lat], axis=1)

    h = direction(0, False) + direction(1, True)
    return (h * jax.nn.gelu(gate.astype(F32))).astype(x.dtype)


def token_mixer(h, n_ctx, need_ctx, rope_cos, rope_sin, w_in, lb_fwd, lb_bwd, hg_norm_w,
                q_norm_w, k_norm_w, conv_w, conv_b, lru_wa, lru_ba, lru_wx, lru_bx, lru_lambda,
                w_br_a, w_br_b, w_br_c, w_out):
    u = h @ w_in
    (a_q, a_ff, a_fb, a_v, a_g, b_q, b_k, b_v, c_x, c_g, merge_logits) = split_cols(u, IN_SIZES)
    y_a = hgrn2_mixer(a_q, a_ff, a_fb, a_v, a_g, n_ctx, lb_fwd, lb_bwd, hg_norm_w)
    y_b = gqa_mixer(b_q, b_k, b_v, n_ctx, rope_cos, rope_sin, q_norm_w, k_norm_w, need_ctx)
    y_c = rglru_mixer(c_x, c_g, n_ctx, conv_w, conv_b, lru_wa, lru_ba, lru_wx, lru_bx, lru_lambda)
    if not need_ctx:
        y_a, y_c, merge_logits = y_a[:, n_ctx:], y_c[:, n_ctx:], merge_logits[:, n_ctx:]
    g_a, g_b, g_c = jnp.split(jax.nn.sigmoid(merge_logits), N_BRANCHES, axis=-1)
    merged = g_a * (y_a @ w_br_a) + g_b * (y_b @ w_br_b) + g_c * (y_c @ w_br_c)
    return merged @ w_out


def swiglu(h, w_gate, w_up, w_down):
    return (jax.nn.silu(h @ w_gate) * (h @ w_up)) @ w_down


def moe_swiglu(h, router_w, w_gate, w_up, w_down):
    logits = (h @ router_w).astype(F32)
    top_v, top_i = lax.top_k(logits, TOP_K)
    top_w = jax.nn.softmax(top_v, axis=-1)
    combine = jnp.sum(jax.nn.one_hot(top_i, N_EXPERTS, dtype=F32) * top_w[..., None], axis=-2)
    out = jnp.zeros(h.shape[:-1] + (w_down.shape[-1],), h.dtype)
    for e in range(N_EXPERTS):
        out = out + combine[..., e:e + 1].astype(h.dtype) * swiglu(h, w_gate[e], w_up[e], w_down[e])
    return out


def setup_inputs(seed: int = 0) -> dict:
    key = jax.random.key(seed)
    ks = iter(jax.random.split(key, 40))
    D, L = D_MODEL, DEPTH
    n_dense, n_moe = (DEPTH + 1) // 2, DEPTH // 2

    def nrm(shape, scale):
        return jax.random.normal(next(ks), shape, F32) * scale

    u = jax.random.uniform(next(ks), (L, 2, LRU_WIDTH), F32, minval=0.9, maxval=0.999)
    return {
        'x': nrm((BATCH, SEQ, D), 1.0),
        'c': nrm((BATCH, D), 1.0),
        'ctx': nrm((BATCH, CTX_LEN, D), 1.0),
        'c_ctx': nrm((D,), 1.0),
        'ada_w': nrm((L, D, 6 * D), 0.5 * D ** -0.5),
        'ada_b': nrm((L, 6 * D), 0.02),
        'mix_norm_w': 1.0 + nrm((L, D), 0.02),
        'ffn_norm_w': 1.0 + nrm((L, D), 0.02),
        'w_in': nrm((L, D, IN_DIM), D ** -0.5),
        'hg_lb_logits': nrm((L, 2, HG_KDIM), 0.5),
        'hg_norm_w': 1.0 + nrm((L, HG_WIDTH), 0.02),
        'q_norm_w': 1.0 + nrm((L, HEAD_DIM), 0.02),
        'k_norm_w': 1.0 + nrm((L, HEAD_DIM), 0.02),
        'lru_conv_w': nrm((L, CONV_W, LRU_WIDTH), CONV_W ** -0.5),
        'lru_conv_b': nrm((L, LRU_WIDTH), 0.02),
        'lru_wa': nrm((L, 2, LRU_BLOCKS, LRU_BLOCK_W, LRU_BLOCK_W), LRU_BLOCK_W ** -0.5),
        'lru_ba': nrm((L, 2, LRU_WIDTH), 0.02),
        'lru_wx': nrm((L, 2, LRU_BLOCKS, LRU_BLOCK_W, LRU_BLOCK_W), LRU_BLOCK_W ** -0.5),
        'lru_bx': nrm((L, 2, LRU_WIDTH), 0.02),
        'lru_lambda': jnp.log(u) - jnp.log1p(-u),
        'w_br_a': nrm((L, HG_WIDTH, D), HG_WIDTH ** -0.5),
        'w_br_b': nrm((L, ATT_WIDTH, D), ATT_WIDTH ** -0.5),
        'w_br_c': nrm((L, LRU_WIDTH, D), LRU_WIDTH ** -0.5),
        'w_out': nrm((L, D, D), D ** -0.5),
        'ffn_w_gate': nrm((n_dense, D, D_FF), D ** -0.5),
        'ffn_w_up': nrm((n_dense, D, D_FF), D ** -0.5),
        'ffn_w_down': nrm((n_dense, D_FF, D), D_FF ** -0.5),
        'router_w': nrm((n_moe, D, N_EXPERTS), D ** -0.5),
        'moe_w_gate': nrm((n_moe, N_EXPERTS, D, D_FF), D ** -0.5),
        'moe_w_up': nrm((n_moe, N_EXPERTS, D, D_FF), D ** -0.5),
        'moe_w_down': nrm((n_moe, N_EXPERTS, D_FF, D), D_FF ** -0.5),
        'final_norm_w': 1.0 + nrm((D,), 0.02),
    }


def reference(x, c, ctx, c_ctx, ada_w, ada_b, mix_norm_w, ffn_norm_w, w_in, hg_lb_logits, hg_norm_w,
              q_norm_w, k_norm_w, lru_conv_w, lru_conv_b, lru_wa, lru_ba, lru_wx, lru_bx, lru_lambda,
              w_br_a, w_br_b, w_br_c, w_out, ffn_w_gate, ffn_w_up, ffn_w_down, router_w,
              moe_w_gate, moe_w_up, moe_w_down, final_norm_w):
    n_lat = x.shape[1]
    n_ctx_tok = ctx.shape[1]
    rows = n_lat // GRID_W
    rope_cos, rope_sin = axial_rope_tables(rows)
    lbs = hgrn2_lower_bounds(hg_lb_logits)
    xs = jnp.concatenate([ctx.astype(x.dtype), x], axis=1)
    n_ctx = n_ctx_tok
    for i in range(DEPTH):
        last = i == DEPTH - 1
        sh1_l, sc1_l, g1_l, sh2_l, sc2_l, g2_l = jnp.split(jax.nn.silu(c) @ ada_w[i] + ada_b[i], 6, axis=-1)
        sh1_c, sc1_c, g1_c, sh2_c, sc2_c, g2_c = jnp.split(jax.nn.silu(c_ctx) @ ada_w[i] + ada_b[i], 6, axis=-1)
        h = (rms_norm(xs, mix_norm_w[i]) * (1.0 + per_token(sc1_c, sc1_l, n_ctx, n_lat))
             + per_token(sh1_c, sh1_l, n_ctx, n_lat))
        y = token_mixer(h, n_ctx, not last, rope_cos, rope_sin, w_in[i], lbs[i, 0], lbs[i, 1], hg_norm_w[i],
                        q_norm_w[i], k_norm_w[i], lru_conv_w[i], lru_conv_b[i], lru_wa[i], lru_ba[i],
                        lru_wx[i], lru_bx[i], lru_lambda[i], w_br_a[i], w_br_b[i], w_br_c[i], w_out[i])
        if last:
            xs = xs[:, n_ctx:]
            n_ctx = 0
        xs = xs + per_token(g1_c, g1_l, n_ctx, n_lat) * y
        h = (rms_norm(xs, ffn_norm_w[i]) * (1.0 + per_token(sc2_c, sc2_l, n_ctx, n_lat))
             + per_token(sh2_c, sh2_l, n_ctx, n_lat))
        if i % 2 == 0:
            f = swiglu(h, ffn_w_gate[i // 2], ffn_w_up[i // 2], ffn_w_down[i // 2])
        else:
            f = moe_swiglu(h, router_w[i // 2], moe_w_gate[i // 2], moe_w_up[i // 2], moe_w_down[i // 2])
        xs = xs + per_token(g2_c, g2_l, n_ctx, n_lat) * f
    return rms_norm(xs, final_norm_w)
```

```python
import functools

import numpy as np
import jax
import jax.numpy as jnp
from jax import lax
from jax.experimental import pallas as pl
from jax.experimental.pallas import tpu as pltpu

F32 = jnp.float32
BF16 = jnp.bfloat16

D_MODEL = 1024
DEPTH = 4
GRID_W = 64
HG_HEADS = 4
HG_DK = 128
HG_WIDTH = 512
ATT_HEADS = 8
ATT_KV_HEADS = 2
HEAD_DIM = 64
ATT_WIDTH = 512
ROPE_THETA = 10000.0
LRU_WIDTH = 512
LRU_BLOCK_W = 64
CONV_W = 4
RG_C = 8.0
D_FF = 2816
N_EXPERTS = 8
EPS = 1e-6

LANES = 128
SUBLANES = 8
TM = 256
HG_CHUNK = 64
HG_SUB = 8
ATT_TK = 256
FF_CHUNK = 256
VMEM_LIMIT = 56 * 1024 * 1024

OFF_MERGE = 0
OFF_HG = 3 * D_MODEL
OFF_LRU = OFF_HG + 5 * HG_WIDTH
OFF_ATT_Q = OFF_LRU + 2 * LRU_WIDTH
OFF_ATT_K = OFF_ATT_Q + ATT_WIDTH
OFF_ATT_V = OFF_ATT_K + ATT_KV_HEADS * HEAD_DIM
IN_DIM = OFF_ATT_V + ATT_KV_HEADS * HEAD_DIM


def _dot(a, b):
    return jnp.dot(a, b, preferred_element_type=F32)


def _dot_nt(a, b):
    return lax.dot_general(a, b, (((1,), (1,)), ((), ())), preferred_element_type=F32)


def _dot_tn(a, b):
    return lax.dot_general(a, b, (((0,), (0,)), ((), ())), preferred_element_type=F32)


def _sigmoid(x):
    return 1.0 / (1.0 + jnp.exp(-x))


def _silu(x):
    return x * _sigmoid(x)


def _params(*sem):
    return pltpu.CompilerParams(dimension_semantics=sem, vmem_limit_bytes=VMEM_LIMIT)


def _mod_row(i, tiles_per_batch, ctx_tiles):
    if ctx_tiles == 0:
        return i // tiles_per_batch
    return jnp.where(i % tiles_per_batch < ctx_tiles, 4, i // tiles_per_batch)


def _norm_mod(x, nw, sc, sh):
    ms = jnp.mean(x * x, axis=-1, keepdims=True)
    return (x * lax.rsqrt(ms + EPS) * nw) * (1.0 + sc) + sh


def _ada_kernel(c_ref, w_ref, b_ref, o_ref):
    s = _silu(c_ref[...])
    o_ref[...] = _dot(s.astype(BF16), w_ref[...].astype(BF16)) + b_ref[...]


def _ada_mods(cvec, ada_w, ada_b):
    depth, d, n = ada_w.shape
    tn = 1536
    return pl.pallas_call(
        _ada_kernel,
        grid=(depth, n // tn),
        in_specs=[
            pl.BlockSpec((SUBLANES, d), lambda l, j: (0, 0)),
            pl.BlockSpec((None, d, tn), lambda l, j: (l, 0, j)),
            pl.BlockSpec((None, 1, tn), lambda l, j: (l, 0, j)),
        ],
        out_specs=pl.BlockSpec((None, SUBLANES, tn), lambda l, j: (l, 0, j)),
        out_shape=jax.ShapeDtypeStruct((depth, SUBLANES, n), F32),
        compiler_params=_params("parallel", "parallel"),
    )(cvec, ada_w, ada_b.reshape(depth, 1, n))


def _in_proj_kernel(x_ref, nw_ref, sc_ref, sh_ref, w_ref, o_ref):
    hb = _norm_mod(x_ref[...], nw_ref[...], sc_ref[...], sh_ref[...]).astype(BF16)
    n = w_ref.shape[1]
    step = 512
    for c0 in range(0, n, step):
        cw = min(step, n - c0)
        o_ref[:, c0:c0 + cw] = _dot(hb, w_ref[:, c0:c0 + cw]).astype(o_ref.dtype)


def _in_proj(xs, nw, mods, w, tiles_per_batch, ctx_tiles):
    m, d = xs.shape
    n = w.shape[1]
    row = functools.partial(_mod_row, tiles_per_batch=tiles_per_batch, ctx_tiles=ctx_tiles)
    return pl.pallas_call(
        _in_proj_kernel,
        grid=(m // TM,),
        in_specs=[
            pl.BlockSpec((TM, d), lambda i: (i, 0)),
            pl.BlockSpec((1, d), lambda i: (0, 0)),
            pl.BlockSpec((None, 1, d), lambda i: (row(i), 0, 1)),
            pl.BlockSpec((None, 1, d), lambda i: (row(i), 0, 0)),
            pl.BlockSpec((d, n), lambda i: (0, 0)),
        ],
        out_specs=pl.BlockSpec((TM, n), lambda i: (i, 0)),
        out_shape=jax.ShapeDtypeStruct((m, n), F32),
        compiler_params=_params("parallel"),
    )(xs, nw.reshape(1, d), mods, mods, w)


def _cumsum_rows(x, reverse):
    n = x.shape[0]
    row = lax.broadcasted_iota(jnp.int32, x.shape, 0)
    s = 1
    while s < n:
        if reverse:
            x = x + jnp.where(row < n - s, pltpu.roll(x, n - s, 0), 0.0)
        else:
            x = x + jnp.where(row >= s, pltpu.roll(x, s, 0), 0.0)
        s *= 2
    return x


def _bcast_row(x, r, rows):
    return jnp.broadcast_to(x[r:r + 1, :], (rows, x.shape[1]))


def _hg_chunk_head(qh, kh, vh, cum, st, reverse, masks, ones_rhs):
    c = qh.shape[0]
    last = cum[0:1, :] if reverse else cum[c - 1:c, :]
    o = _dot_nt((qh * jnp.exp(cum)).astype(BF16), st.astype(BF16))
    kd = kh * jnp.exp(last - cum)
    st_new = st * jnp.exp(last) + _dot_tn(vh.astype(BF16), kd.astype(BF16))

    a = jnp.zeros((c, c), F32)
    half = c // 2
    li = 0
    while half >= HG_SUB:
        blk = 2 * half
        refs = []
        for b0 in range(0, c, blk):
            r = b0 + half if reverse else b0 + half - 1
            refs.append(_bcast_row(cum, r, blk))
        ref = refs[0] if len(refs) == 1 else jnp.concatenate(refs, axis=0)
        qs = qh * jnp.exp(jnp.minimum(cum - ref, 0.0))
        ks = kh * jnp.exp(jnp.minimum(ref - cum, 0.0))
        a = a + jnp.where(masks[li], _dot_nt(qs.astype(BF16), ks.astype(BF16)), 0.0)
        half //= 2
        li += 1

    rsub = lax.broadcasted_iota(jnp.int32, (HG_SUB, LANES), 0)
    pieces = []
    for g in range(c // HG_SUB):
        r0 = g * HG_SUB
        qg = qh[r0:r0 + HG_SUB, :]
        cg = cum[r0:r0 + HG_SUB, :]
        for j in range(HG_SUB):
            cs = _bcast_row(cum, r0 + j, HG_SUB)
            ks = _bcast_row(kh, r0 + j, HG_SUB)
            p = qg * ks * jnp.exp(jnp.minimum(cg - cs, 0.0))
            valid = (rsub <= j) if reverse else (rsub >= j)
            pieces.append(jnp.where(valid, p, 0.0))
    rs = _dot(jnp.concatenate(pieces, axis=0).astype(BF16), ones_rhs)
    lane = lax.broadcasted_iota(jnp.int32, (HG_SUB, c), 1)
    diag = []
    for g in range(c // HG_SUB):
        acc = jnp.zeros((HG_SUB, c), F32)
        for j in range(HG_SUB):
            idx = g * HG_SUB + j
            acc = jnp.where(lane == idx, rs[idx * HG_SUB:(idx + 1) * HG_SUB, :], acc)
        diag.append(acc)
    a = a + jnp.concatenate(diag, axis=0)
    o = o + _dot(a.astype(BF16), vh.astype(BF16))
    return o, st_new


def _hg_kernel(*refs, reverse, final, rows):
    if final:
        q_ref, f_ref, v_ref, la_ref, lb_ref, g_ref, op_ref, nw_ref, o_ref, s_ref = refs
    else:
        q_ref, f_ref, v_ref, la_ref, lb_ref, o_ref, s_ref = refs

    @pl.when(pl.program_id(1) == 0)
    def _():
        s_ref[...] = jnp.zeros_like(s_ref)

    c = HG_CHUNK
    ti = lax.broadcasted_iota(jnp.int32, (c, c), 0)
    si = lax.broadcasted_iota(jnp.int32, (c, c), 1)
    masks = []
    half = c // 2
    while half >= HG_SUB:
        blk = 2 * half
        same = (ti & -blk) == (si & -blk)
        if reverse:
            m = same & ((ti & half) == 0) & ((si & half) != 0)
        else:
            m = same & ((ti & half) != 0) & ((si & half) == 0)
        masks.append(m)
        half //= 2
    ones_rhs = jnp.ones((LANES, c), BF16)

    n_ch = rows // c
    order = range(n_ch - 1, -1, -1) if reverse else range(n_ch)
    for ci in order:
        r0 = ci * c
        z = f_ref[r0:r0 + c, :]
        ls = jnp.minimum(z, 0.0) - jnp.log1p(jnp.exp(-jnp.abs(z)))
        bt = lb_ref[...] + ls
        at = jnp.broadcast_to(la_ref[...], bt.shape)
        logf = jnp.maximum(at, bt) + jnp.log1p(jnp.exp(-jnp.abs(at - bt)))
        kf = 1.0 - jnp.exp(logf)
        qf = _silu(q_ref[r0:r0 + c, :])
        vv = v_ref[r0:r0 + c, :]
        cum = _cumsum_rows(logf, reverse)
        outs = []
        for hh in range(HG_HEADS):
            sl = slice(hh * HG_DK, (hh + 1) * HG_DK)
            o, st_new = _hg_chunk_head(qf[:, sl], kf[:, sl], vv[:, sl], cum[:, sl], s_ref[hh],
                                       reverse, masks, ones_rhs)
            s_ref[hh] = st_new
            if final:
                tot = o + op_ref[r0:r0 + c, sl]
                ms = jnp.mean(tot * tot, axis=-1, keepdims=True)
                o = tot * lax.rsqrt(ms + EPS) * nw_ref[:, sl]
            outs.append(o)
        o_all = jnp.concatenate(outs, axis=1)
        if final:
            o_all = o_all * _silu(g_ref[r0:r0 + c, :])
        o_ref[r0:r0 + c, :] = o_all.astype(o_ref.dtype)


def _hg_direction(u, la, lb, batch, n_ctx, reverse, o_prev=None, nw=None):
    m = u.shape[0]
    t = m // batch
    nblk = t // TM
    ncb = n_ctx // TM
    final = o_prev is not None
    cb = OFF_HG // HG_WIDTH

    def blk(j):
        if not reverse:
            return j
        return jnp.where(j < ncb, ncb - 1 - j, nblk - 1 - (j - ncb))

    def tok(col):
        return pl.BlockSpec((TM, HG_WIDTH), lambda b, j: (b * nblk + blk(j), col))

    vec = pl.BlockSpec((1, HG_WIDTH), lambda b, j: (0, 0))
    f_col = cb + 2 if reverse else cb + 1
    in_specs = [tok(cb), tok(f_col), tok(cb + 3), vec, vec]
    args = [u, u, u, la.reshape(1, -1), lb.reshape(1, -1)]
    if final:
        in_specs += [tok(cb + 4), tok(0), vec]
        args += [u, o_prev, nw.reshape(1, -1)]
    return pl.pallas_call(
        functools.partial(_hg_kernel, reverse=reverse, final=final, rows=TM),
        grid=(batch, nblk),
        in_specs=in_specs,
        out_specs=tok(0),
        out_shape=jax.ShapeDtypeStruct((m, HG_WIDTH), BF16 if final else F32),
        scratch_shapes=[pltpu.VMEM((HG_HEADS, HG_DK, HG_DK), F32)],
        compiler_params=_params("parallel", "arbitrary"),
    )(*args)


def _swap_halves(x):
    lane = lax.broadcasted_iota(jnp.int32, x.shape, 1)
    return jnp.where((lane & 32) == 0, pltpu.roll(x, LANES - 32, 1), pltpu.roll(x, 32, 1))


def _head_norm_rope(x, w, cos, sin, gmat):
    x2 = x * x
    hi = x2.astype(BF16)
    lo = (x2 - hi.astype(F32)).astype(BF16)
    ms = (_dot(hi, gmat) + _dot(lo, gmat)) * (1.0 / HEAD_DIM)
    xn = x * lax.rsqrt(ms + EPS) * w
    return xn * cos + _swap_halves(xn) * sin


def _att_prep_kernel(q_ref, k_ref, v_ref, cos_ref, sin_ref, qw_ref, kw_ref, g_ref,
                     qo_ref, k0_ref, k1_ref, vo_ref):
    cos, sin, gmat = cos_ref[...], sin_ref[...], g_ref[...]
    for j in range(ATT_WIDTH // LANES):
        sl = slice(j * LANES, (j + 1) * LANES)
        qr = _head_norm_rope(q_ref[:, sl], qw_ref[...], cos, sin, gmat)
        qo_ref[:, sl] = (qr * (HEAD_DIM ** -0.5)).astype(BF16)
    kr = _head_norm_rope(k_ref[...], kw_ref[...], cos, sin, gmat)
    lane = lax.broadcasted_iota(jnp.int32, kr.shape, 1)
    k0_ref[...] = jnp.where(lane < HEAD_DIM, kr, 0.0).astype(BF16)
    k1_ref[...] = jnp.where(lane >= HEAD_DIM, kr, 0.0).astype(BF16)
    vo_ref[...] = v_ref[...].astype(BF16)


def _att_prep(u, cos, sin, qw, kw, gmat, batch):
    m = u.shape[0]
    tpb = m // batch // TM
    kvw = ATT_KV_HEADS * HEAD_DIM
    tab = pl.BlockSpec((TM, LANES), lambda i: (i % tpb, 0))
    vec = pl.BlockSpec((1, LANES), lambda i: (0, 0))
    kv_out = pl.BlockSpec((TM, kvw), lambda i: (i, 0))
    return pl.pallas_call(
        _att_prep_kernel,
        grid=(m // TM,),
        in_specs=[
            pl.BlockSpec((TM, ATT_WIDTH), lambda i: (i, OFF_ATT_Q // ATT_WIDTH)),
            pl.BlockSpec((TM, kvw), lambda i: (i, OFF_ATT_K // kvw)),
            pl.BlockSpec((TM, kvw), lambda i: (i, OFF_ATT_V // kvw)),
            tab, tab, vec, vec,
            pl.BlockSpec((LANES, LANES), lambda i: (0, 0)),
        ],
        out_specs=[pl.BlockSpec((TM, ATT_WIDTH), lambda i: (i, 0)), kv_out, kv_out, kv_out],
        out_shape=[jax.ShapeDtypeStruct((m, ATT_WIDTH), BF16)] + [jax.ShapeDtypeStruct((m, kvw), BF16)] * 3,
        compiler_params=_params("parallel"),
    )(u, u, u, cos, sin, qw, kw, gmat)


def _att_kernel(q_ref, k0_ref, k1_ref, v_ref, o_ref, *, ctx_q_tiles, n_ctx, n_all):
    q = q_ref[...]
    tq = q.shape[0]
    nk = jnp.where(pl.program_id(2) < ctx_q_tiles, n_ctx // ATT_TK, n_all // ATT_TK)

    def step(s, v, carry):
        m, l, acc = carry
        m_new = jnp.maximum(m, jnp.max(s, axis=-1, keepdims=True))
        p = jnp.exp(s - m_new)
        alpha = jnp.exp(m - m_new)
        l = alpha * l + jnp.sum(p, axis=-1, keepdims=True)
        acc = alpha * acc + _dot(p.astype(BF16), v)
        return m_new, l, acc

    def body(kt, carry):
        ca, cb = carry
        r0 = pl.multiple_of(kt * ATT_TK, ATT_TK)
        v = v_ref[pl.ds(r0, ATT_TK), :]
        ca = step(_dot_nt(q, k0_ref[pl.ds(r0, ATT_TK), :]), v, ca)
        cb = step(_dot_nt(q, k1_ref[pl.ds(r0, ATT_TK), :]), v, cb)
        return ca, cb

    init = (jnp.full((tq, 1), -jnp.inf, F32), jnp.zeros((tq, 1), F32), jnp.zeros((tq, LANES), F32))
    (_, la, acca), (_, lb, accb) = lax.fori_loop(0, nk, body, (init, init))
    lane = lax.broadcasted_iota(jnp.int32, (tq, LANES), 1)
    o_ref[...] = jnp.where(lane < HEAD_DIM, acca / la, accb / lb).astype(o_ref.dtype)


def _attention(qr, k0, k1, vv, batch, n_ctx):
    m = qr.shape[0]
    t = m // batch
    nq = t // TM
    kvw = ATT_KV_HEADS * HEAD_DIM
    qspec = pl.BlockSpec((TM, LANES), lambda b, j, i: (b * nq + i, j))
    kvspec = pl.BlockSpec((t, kvw), lambda b, j, i: (b, 0))
    return pl.pallas_call(
        functools.partial(_att_kernel, ctx_q_tiles=n_ctx // TM, n_ctx=n_ctx, n_all=t),
        grid=(batch, ATT_WIDTH // LANES, nq),
        in_specs=[qspec, kvspec, kvspec, kvspec],
        out_specs=qspec,
        out_shape=jax.ShapeDtypeStruct((m, ATT_WIDTH), BF16),
        compiler_params=_params("parallel", "parallel", "parallel"),
    )(qr, k0, k1, vv)


def _scan_rows(a, u, reverse):
    n = a.shape[0]
    row = lax.broadcasted_iota(jnp.int32, a.shape, 0)
    s = 1
    while s < n:
        if reverse:
            keep = row < n - s
            ash, ush = pltpu.roll(a, n - s, 0), pltpu.roll(u, n - s, 0)
        else:
            keep = row >= s
            ash, ush = pltpu.roll(a, s, 0), pltpu.roll(u, s, 0)
        u = u + a * jnp.where(keep, ush, 0.0)
        a = a * jnp.where(keep, ash, 1.0)
        s *= 2
    return a, u


def _lru_kernel(x_ref, g_ref, cw_ref, cb_ref, w_ref, b_ref, lam_ref, o_ref, xp_ref, xc_ref, h_ref,
                *, n_ctx, n_tok):
    tt = TM
    pad = SUBLANES
    n_lat = n_tok - n_ctx
    lat0 = n_ctx + 2 * pad
    zeros = jnp.zeros((pad, LANES), F32)
    xp_ref[0:pad, :] = zeros
    xp_ref[pad + n_ctx:lat0, :] = zeros
    xp_ref[lat0 + n_lat:lat0 + n_lat + pad, :] = zeros
    xp_ref[pad:pad + n_ctx, :] = x_ref[0:n_ctx, :]
    xp_ref[lat0:lat0 + n_lat, :] = x_ref[n_ctx:n_tok, :]

    n_tiles = n_tok // tt
    ctx_tiles = n_ctx // tt

    def conv_tile(i, _):
        r0 = pl.multiple_of(i * tt, tt)
        p0 = pl.multiple_of(jnp.where(i < ctx_tiles, r0, r0 + pad), pad)
        ext = xp_ref[pl.ds(p0, tt + 2 * pad), :]
        acc = cb_ref[...] + cw_ref[0:1, :] * ext[pad - 2:pad - 2 + tt, :]
        for j in range(1, CONV_W):
            acc = acc + cw_ref[j:j + 1, :] * ext[pad - 2 + j:pad - 2 + j + tt, :]
        xc_ref[pl.ds(r0, tt), :] = acc
        return 0

    lax.fori_loop(0, n_tiles, conv_tile, 0)

    lam = lam_ref[...]
    sp = jnp.maximum(-lam, 0.0) + jnp.log1p(jnp.exp(-jnp.abs(lam)))

    def coeffs(r0, d):
        xc = xc_ref[pl.ds(r0, tt), :]
        pre = _dot(xc.astype(BF16), w_ref[:, d * 2 * LANES:(d + 1) * 2 * LANES]) \
            + b_ref[:, d * 2 * LANES:(d + 1) * 2 * LANES]
        r = _sigmoid(pre[:, :LANES])
        ig = _sigmoid(pre[:, LANES:])
        log_a = (-RG_C) * r * sp[d:d + 1, :]
        a = jnp.exp(log_a)
        return a, jnp.sqrt(1.0 - a * a) * (ig * xc)

    def fwd_tile(i, carry):
        r0 = pl.multiple_of(i * tt, tt)
        a, u = coeffs(r0, 0)
        a, u = _scan_rows(a, u, False)
        h = u + a * carry
        h_ref[pl.ds(r0, tt), :] = h
        return h[tt - 1:tt, :]

    lax.fori_loop(0, n_tiles, fwd_tile, jnp.zeros((1, LANES), F32))

    def bwd_tile(j, carry):
        i = jnp.where(j < ctx_tiles, ctx_tiles - 1 - j, n_tiles - 1 - (j - ctx_tiles))
        r0 = pl.multiple_of(i * tt, tt)
        a, u = coeffs(r0, 1)
        a, u = _scan_rows(a, u, True)
        h = u + a * carry
        tot = h + h_ref[pl.ds(r0, tt), :]
        o_ref[pl.ds(r0, tt), :] = (tot * jax.nn.gelu(g_ref[pl.ds(r0, tt), :])).astype(o_ref.dtype)
        return h[0:1, :]

    lax.fori_loop(0, n_tiles, bwd_tile, jnp.zeros((1, LANES), F32))


def _rglru(u, conv_w, conv_b, wcat, bcat, lam, batch, n_ctx):
    m = u.shape[0]
    t = m // batch
    ncg = LRU_WIDTH // LANES
    xcol = OFF_LRU // LANES
    return pl.pallas_call(
        functools.partial(_lru_kernel, n_ctx=n_ctx, n_tok=t),
        grid=(batch, ncg),
        in_specs=[
            pl.BlockSpec((t, LANES), lambda b, c: (b, xcol + c)),
            pl.BlockSpec((t, LANES), lambda b, c: (b, xcol + ncg + c)),
            pl.BlockSpec((CONV_W, LANES), lambda b, c: (0, c)),
            pl.BlockSpec((1, LANES), lambda b, c: (0, c)),
            pl.BlockSpec((None, LANES, 4 * LANES), lambda b, c: (c, 0, 0)),
            pl.BlockSpec((None, 1, 4 * LANES), lambda b, c: (c, 0, 0)),
            pl.BlockSpec((2, LANES), lambda b, c: (0, c)),
        ],
        out_specs=pl.BlockSpec((t, LANES), lambda b, c: (b, c)),
        out_shape=jax.ShapeDtypeStruct((m, LRU_WIDTH), BF16),
        scratch_shapes=[pltpu.VMEM((t + 3 * SUBLANES, LANES), F32),
                        pltpu.VMEM((t, LANES), F32),
                        pltpu.VMEM((t, LANES), F32)],
        compiler_params=_params("parallel", "parallel"),
    )(u, u, conv_w, conv_b.reshape(1, -1), wcat, bcat, lam)


def _merge_kernel(ya_ref, yb_ref, yc_ref, ml_ref, xs_ref, g1_ref, wa_ref, wb_ref, wc_ref, wo_ref, o_ref):
    d = D_MODEL
    acc = _sigmoid(ml_ref[:, 0:d]) * _dot(ya_ref[...], wa_ref[...])
    acc = acc + _sigmoid(ml_ref[:, d:2 * d]) * _dot(yb_ref[...], wb_ref[...])
    acc = acc + _sigmoid(ml_ref[:, 2 * d:3 * d]) * _dot(yc_ref[...], wc_ref[...])
    y = _dot(acc.astype(BF16), wo_ref[...])
    o_ref[...] = xs_ref[...] + g1_ref[...] * y


def _merge(ya, yb, yc, u, xs, mods, wa, wb, wc, wo, batch, n_ctx, drop_ctx):
    m, d = xs.shape
    tpb = m // batch // TM
    ctx_tiles = n_ctx // TM
    if drop_ctx:
        out_tpb = tpb - ctx_tiles
        src = lambda i: (i // out_tpb) * tpb + ctx_tiles + i % out_tpb
        row = lambda i: i // out_tpb
        m_out = batch * out_tpb * TM
    else:
        src = lambda i: i
        row = functools.partial(_mod_row, tiles_per_batch=tpb, ctx_tiles=ctx_tiles)
        m_out = m
    br = pl.BlockSpec((TM, HG_WIDTH), lambda i: (src(i), 0))
    full = lambda a: pl.BlockSpec(a.shape, lambda i: (0, 0))
    return pl.pallas_call(
        _merge_kernel,
        grid=(m_out // TM,),
        in_specs=[br, br, br,
                  pl.BlockSpec((TM, 3 * d), lambda i: (src(i), OFF_MERGE)),
                  pl.BlockSpec((TM, d), lambda i: (src(i), 0)),
                  pl.BlockSpec((None, 1, d), lambda i: (row(i), 0, 2)),
                  full(wa), full(wb), full(wc), full(wo)],
        out_specs=pl.BlockSpec((TM, d), lambda i: (i, 0)),
        out_shape=jax.ShapeDtypeStruct((m_out, d), F32),
        compiler_params=_params("parallel"),
    )(ya, yb, yc, u, xs, mods, wa, wb, wc, wo)


def _swiglu_acc(hb, wg_ref, wu_ref, wd_ref):
    acc = jnp.zeros((hb.shape[0], wd_ref.shape[1]), F32)
    for c0 in range(0, D_FF, FF_CHUNK):
        a = _dot(hb, wg_ref[:, c0:c0 + FF_CHUNK])
        b = _dot(hb, wu_ref[:, c0:c0 + FF_CHUNK])
        acc = acc + _dot((_silu(a) * b).astype(BF16), wd_ref[c0:c0 + FF_CHUNK, :])
    return acc


def _final_norm(y, fw):
    ms = jnp.mean(y * y, axis=-1, keepdims=True)
    return y * lax.rsqrt(ms + EPS) * fw


def _ffn_kernel(x_ref, nw_ref, sc_ref, sh_ref, g2_ref, wg_ref, wu_ref, wd_ref, fw_ref, o_ref, *, final):
    x = x_ref[...]
    hb = _norm_mod(x, nw_ref[...], sc_ref[...], sh_ref[...]).astype(BF16)
    y = x + g2_ref[...] * _swiglu_acc(hb, wg_ref, wu_ref, wd_ref)
    o_ref[...] = _final_norm(y, fw_ref[...]) if final else y


def _mod_specs(row, d):
    return [pl.BlockSpec((None, 1, d), lambda i, *_: (row(i), 0, 4)),
            pl.BlockSpec((None, 1, d), lambda i, *_: (row(i), 0, 3)),
            pl.BlockSpec((None, 1, d), lambda i, *_: (row(i), 0, 5))]


def _ffn(xs, nw, mods, wg, wu, wd, fw, tiles_per_batch, ctx_tiles, final):
    m, d = xs.shape
    row = functools.partial(_mod_row, tiles_per_batch=tiles_per_batch, ctx_tiles=ctx_tiles)
    full = lambda a: pl.BlockSpec(a.shape, lambda i: (0, 0))
    vec = pl.BlockSpec((1, d), lambda i: (0, 0))
    return pl.pallas_call(
        functools.partial(_ffn_kernel, final=final),
        grid=(m // TM,),
        in_specs=[pl.BlockSpec((TM, d), lambda i: (i, 0)), vec] + _mod_specs(row, d)
                 + [full(wg), full(wu), full(wd), vec],
        out_specs=pl.BlockSpec((TM, d), lambda i: (i, 0)),
        out_shape=jax.ShapeDtypeStruct((m, d), F32),
        compiler_params=_params("parallel"),
    )(xs, nw.reshape(1, d), mods, mods, mods, wg, wu, wd, fw.reshape(1, d))


def _top2_combine(logits):
    lane = lax.broadcasted_iota(jnp.int32, logits.shape, 1)
    m1 = jnp.max(logits, axis=-1, keepdims=True)
    i1 = jnp.min(jnp.where(logits == m1, lane, LANES), axis=-1, keepdims=True)
    rest = jnp.where(lane == i1, -jnp.inf, logits)
    m2 = jnp.max(rest, axis=-1, keepdims=True)
    i2 = jnp.min(jnp.where(rest == m2, lane, LANES), axis=-1, keepdims=True)
    e2 = jnp.exp(m2 - m1)
    w1 = 1.0 / (1.0 + e2)
    return jnp.where(lane == i1, w1, 0.0) + jnp.where(lane == i2, e2 * w1, 0.0)


def _moe_kernel(x_ref, nw_ref, sc_ref, sh_ref, g2_ref, rw_ref, wg_ref, wu_ref, wd_ref, fw_ref, o_ref,
                hb_ref, comb_ref, acc_ref, *, final):
    e = pl.program_id(1)

    @pl.when(e == 0)
    def _():
        h = _norm_mod(x_ref[...], nw_ref[...], sc_ref[...], sh_ref[...])
        hb_ref[...] = h.astype(BF16)
        logits = jnp.dot(h, rw_ref[...], preferred_element_type=F32, precision=lax.Precision.HIGHEST)
        lane = lax.broadcasted_iota(jnp.int32, logits.shape, 1)
        comb_ref[...] = _top2_combine(jnp.where(lane < N_EXPERTS, logits, -jnp.inf))
        acc_ref[...] = jnp.zeros_like(acc_ref)

    comb = comb_ref[...]
    lane = lax.broadcasted_iota(jnp.int32, comb.shape, 1)
    ce = jnp.sum(jnp.where(lane == e, comb, 0.0), axis=-1, keepdims=True)
    acc_ref[...] += ce * _swiglu_acc(hb_ref[...], wg_ref, wu_ref, wd_ref)

    @pl.when(e == N_EXPERTS - 1)
    def _():
        y = x_ref[...] + g2_ref[...] * acc_ref[...]
        o_ref[...] = _final_norm(y, fw_ref[...]) if final else y


def _moe(xs, nw, mods, rw, wg, wu, wd, fw, tiles_per_batch, ctx_tiles, final):
    m, d = xs.shape
    row = functools.partial(_mod_row, tiles_per_batch=tiles_per_batch, ctx_tiles=ctx_tiles)
    vec = pl.BlockSpec((1, d), lambda i, e: (0, 0))
    return pl.pallas_call(
        functools.partial(_moe_kernel, final=final),
        grid=(m // TM, N_EXPERTS),
        in_specs=[pl.BlockSpec((TM, d), lambda i, e: (i, 0)), vec] + _mod_specs(row, d)
                 + [pl.BlockSpec((d, LANES), lambda i, e: (0, 0)),
                    pl.BlockSpec((None, d, D_FF), lambda i, e: (e, 0, 0)),
                    pl.BlockSpec((None, d, D_FF), lambda i, e: (e, 0, 0)),
                    pl.BlockSpec((None, D_FF, d), lambda i, e: (e, 0, 0)),
                    vec],
        out_specs=pl.BlockSpec((TM, d), lambda i, e: (i, 0)),
        out_shape=jax.ShapeDtypeStruct((m, d), F32),
        scratch_shapes=[pltpu.VMEM((TM, d), BF16), pltpu.VMEM((TM, LANES), F32), pltpu.VMEM((TM, d), F32)],
        compiler_params=_params("parallel", "arbitrary"),
    )(xs, nw.reshape(1, d), mods, mods, mods, rw, wg, wu, wd, fw.reshape(1, d))


def _column_perm():
    sizes = (512, 512, 512, 512, 512, 512, 128, 128, 512, 512, 3 * D_MODEL)
    starts = np.concatenate([[0], np.cumsum(sizes)[:-1]])
    a_q, a_ff, a_fb, a_v, a_g, b_q, b_k, b_v, c_x, c_g, mrg = [
        np.arange(s, s + n) for s, n in zip(starts, sizes)]
    half = np.concatenate([np.arange(0, HEAD_DIM, 2), np.arange(1, HEAD_DIM, 2)])
    group = ATT_HEADS // ATT_KV_HEADS
    q_cols = []
    for j in range(group):
        q_cols += [b_q[j * HEAD_DIM + half], b_q[(group + j) * HEAD_DIM + half]]
    k_cols = [b_k[h * HEAD_DIM + half] for h in range(ATT_KV_HEADS)]
    return np.concatenate([mrg, a_q, a_ff, a_fb, a_v, a_g, c_x, c_g] + q_cols + k_cols + [b_v])


def _att_out_rows():
    group = ATT_HEADS // ATT_KV_HEADS
    idx = []
    for j in range(group):
        idx += [np.arange(j * HEAD_DIM, (j + 1) * HEAD_DIM),
                np.arange((group + j) * HEAD_DIM, (group + j + 1) * HEAD_DIM)]
    return np.concatenate(idx)


def _rope_tables(n_ctx, n_lat):
    rows = n_lat // GRID_W
    row = jnp.repeat(jnp.arange(rows, dtype=F32), GRID_W)
    col = jnp.tile(jnp.arange(GRID_W, dtype=F32), rows)
    pairs = HEAD_DIM // 4
    freqs = ROPE_THETA ** (-jnp.arange(pairs, dtype=F32) / pairs)
    ang = jnp.concatenate([row[:, None] * freqs, col[:, None] * freqs], axis=-1)
    cos, sin = jnp.cos(ang), jnp.sin(ang)
    cos = jnp.concatenate([jnp.ones((n_ctx, HEAD_DIM // 2), F32), cos], axis=0)
    sin = jnp.concatenate([jnp.zeros((n_ctx, HEAD_DIM // 2), F32), sin], axis=0)
    reps = LANES // HEAD_DIM
    return (jnp.tile(jnp.concatenate([cos, cos], axis=-1), (1, reps)),
            jnp.tile(jnp.concatenate([-sin, sin], axis=-1), (1, reps)))


def _lru_weights(wa, wx, ba, bx):
    ncg = LRU_WIDTH // LANES

    def dense(w):
        w = w.reshape(ncg, 2, LRU_BLOCK_W, LRU_BLOCK_W)
        out = jnp.zeros((ncg, LANES, LANES), w.dtype)
        out = out.at[:, :LRU_BLOCK_W, :LRU_BLOCK_W].set(w[:, 0])
        return out.at[:, LRU_BLOCK_W:, LRU_BLOCK_W:].set(w[:, 1])

    wcat = jnp.concatenate([dense(wa[0]), dense(wx[0]), dense(wa[1]), dense(wx[1])], axis=-1).astype(BF16)
    bcat = jnp.concatenate([b.reshape(ncg, 1, LANES) for b in (ba[0], bx[0], ba[1], bx[1])], axis=-1)
    return wcat, bcat


def kernel(x, c, ctx, c_ctx, ada_w, ada_b, mix_norm_w, ffn_norm_w, w_in, hg_lb_logits, hg_norm_w, q_norm_w, k_norm_w, lru_conv_w, lru_conv_b, lru_wa, lru_ba, lru_wx, lru_bx, lru_lambda, w_br_a, w_br_b, w_br_c, w_out, ffn_w_gate, ffn_w_up, ffn_w_down, router_w, moe_w_gate, moe_w_up, moe_w_down, final_norm_w):
    batch, n_lat, d = x.shape
    n_ctx = ctx.shape[1]
    t = n_ctx + n_lat
    depth = ada_w.shape[0]
    assert n_ctx % TM == 0 and n_lat % TM == 0 and batch <= 4

    cvec = jnp.zeros((SUBLANES, d), F32).at[:batch].set(c).at[4].set(c_ctx)
    mods_all = _ada_mods(cvec, ada_w, ada_b).reshape(depth, SUBLANES, 1, 6 * d)

    p = jax.nn.softmax(hg_lb_logits.astype(F32), axis=0)
    cum = jnp.cumsum(p, axis=0)
    lbs = cum - cum[:1]
    log_lb, log_1m_lb = jnp.log(lbs), jnp.log1p(-lbs)

    cos, sin = _rope_tables(n_ctx, n_lat)
    half = np.concatenate([np.arange(0, HEAD_DIM, 2), np.arange(1, HEAD_DIM, 2)])
    gidx = np.arange(LANES) // HEAD_DIM
    gmat = jnp.asarray(gidx[:, None] == gidx[None, :], BF16)
    col_perm = _column_perm()
    att_rows = _att_out_rows()

    xs = jnp.concatenate([ctx.astype(x.dtype), x], axis=1).reshape(batch * t, d)
    tpb, ctx_tiles = t // TM, n_ctx // TM
    for i in range(depth):
        last = i == depth - 1
        mods = mods_all[i]
        w_i = w_in[i][:, col_perm].astype(BF16)
        u = _in_proj(xs, mix_norm_w[i], mods, w_i, tpb, ctx_tiles)

        o_f = _hg_direction(u, log_lb[i, 0], log_1m_lb[i, 0], batch, n_ctx, False)
        y_a = _hg_direction(u, log_lb[i, 1], log_1m_lb[i, 1], batch, n_ctx, True, o_f, hg_norm_w[i])

        reps = LANES // HEAD_DIM
        qw = jnp.tile(q_norm_w[i][half], reps).reshape(1, LANES)
        kw = jnp.tile(k_norm_w[i][half], reps).reshape(1, LANES)
        qr, k0, k1, vv = _att_prep(u, cos, sin, qw, kw, gmat, batch)
        y_b = _attention(qr, k0, k1, vv, batch, n_ctx)

        wcat, bcat = _lru_weights(lru_wa[i], lru_wx[i], lru_ba[i], lru_bx[i])
        y_c = _rglru(u, lru_conv_w[i], lru_conv_b[i], wcat, bcat, lru_lambda[i], batch, n_ctx)

        xs = _merge(y_a, y_b, y_c, u, xs, mods, w_br_a[i].astype(BF16), w_br_b[i][att_rows].astype(BF16),
                    w_br_c[i].astype(BF16), w_out[i].astype(BF16), batch, n_ctx, last)
        if last:
            tpb, ctx_tiles = n_lat // TM, 0
        if i % 2 == 0:
            j = i // 2
            xs = _ffn(xs, ffn_norm_w[i], mods, ffn_w_gate[j].astype(BF16), ffn_w_up[j].astype(BF16),
                      ffn_w_down[j].astype(BF16), final_norm_w, tpb, ctx_tiles, last)
        else:
            j = i // 2
            rw = jnp.zeros((d, LANES), F32).at[:, :N_EXPERTS].set(router_w[j])
            xs = _moe(xs, ffn_norm_w[i], mods, rw, moe_w_gate[j].astype(BF16), moe_w_up[j].astype(BF16),
                      moe_w_down[j].astype(BF16), final_norm_w, tpb, ctx_tiles, last)
    return xs.reshape(batch, n_lat, d)
```

```python
import functools

import numpy as np
import jax
import jax.numpy as jnp
from jax import lax
from jax.experimental import pallas as pl
from jax.experimental.pallas import tpu as pltpu

F32 = jnp.float32
BF16 = jnp.bfloat16

D_MODEL = 1024
DEPTH = 4
GRID_W = 64
HG_HEADS = 4
HG_DK = 128
HG_WIDTH = 512
ATT_HEADS = 8
ATT_KV_HEADS = 2
HEAD_DIM = 64
ATT_WIDTH = 512
ROPE_THETA = 10000.0
LRU_WIDTH = 512
LRU_BLOCK_W = 64
CONV_W = 4
RG_C = 8.0
D_FF = 2816
N_EXPERTS = 8
EPS = 1e-6

LANES = 128
SUBLANES = 8
TM = 256
HG_CHUNK = 64
HG_SUB = 8
ATT_TQ = 512
ATT_SAFE_BOUND = 60.0
FF_CHUNK = 256
VMEM_LIMIT = 56 * 1024 * 1024

OFF_MERGE = 0
OFF_HG = 3 * D_MODEL
OFF_LRU = OFF_HG + 5 * HG_WIDTH
OFF_ATT_Q = OFF_LRU + 2 * LRU_WIDTH
OFF_ATT_K = OFF_ATT_Q + ATT_WIDTH
OFF_ATT_V = OFF_ATT_K + ATT_KV_HEADS * HEAD_DIM
IN_DIM = OFF_ATT_V + ATT_KV_HEADS * HEAD_DIM


def _dot(a, b):
    return jnp.dot(a, b, preferred_element_type=F32)


def _dot_nt(a, b):
    return lax.dot_general(a, b, (((1,), (1,)), ((), ())), preferred_element_type=F32)


def _dot_tn(a, b):
    return lax.dot_general(a, b, (((0,), (0,)), ((), ())), preferred_element_type=F32)


def _sigmoid(x):
    return 1.0 / (1.0 + jnp.exp(-x))


def _silu(x):
    return x * _sigmoid(x)


def _params(*sem):
    return pltpu.CompilerParams(dimension_semantics=sem, vmem_limit_bytes=VMEM_LIMIT)


def _mod_row(i, tiles_per_batch, ctx_tiles):
    if ctx_tiles == 0:
        return i // tiles_per_batch
    return jnp.where(i % tiles_per_batch < ctx_tiles, 4, i // tiles_per_batch)


def _norm_mod(x, nw, sc, sh):
    ms = jnp.mean(x * x, axis=-1, keepdims=True)
    return (x * lax.rsqrt(ms + EPS) * nw) * (1.0 + sc) + sh


def _ada_kernel(c_ref, w_ref, b_ref, o_ref):
    s = _silu(c_ref[...])
    o_ref[...] = _dot(s.astype(BF16), w_ref[...].astype(BF16)) + b_ref[...]


def _ada_mods(cvec, ada_w, ada_b):
    depth, d, n = ada_w.shape
    tn = 1536
    return pl.pallas_call(
        _ada_kernel,
        grid=(depth, n // tn),
        in_specs=[
            pl.BlockSpec((SUBLANES, d), lambda l, j: (0, 0)),
            pl.BlockSpec((None, d, tn), lambda l, j: (l, 0, j)),
            pl.BlockSpec((None, 1, tn), lambda l, j: (l, 0, j)),
        ],
        out_specs=pl.BlockSpec((None, SUBLANES, tn), lambda l, j: (l, 0, j)),
        out_shape=jax.ShapeDtypeStruct((depth, SUBLANES, n), F32),
        compiler_params=_params("parallel", "parallel"),
        name="ada_mods",
    )(cvec, ada_w, ada_b.reshape(depth, 1, n))


def _in_proj_kernel(x_ref, nw_ref, sc_ref, sh_ref, w_ref, o_ref):
    hb = _norm_mod(x_ref[...], nw_ref[...], sc_ref[...], sh_ref[...]).astype(BF16)
    n = w_ref.shape[1]
    step = 512
    for c0 in range(0, n, step):
        cw = min(step, n - c0)
        o_ref[:, c0:c0 + cw] = _dot(hb, w_ref[:, c0:c0 + cw]).astype(o_ref.dtype)


def _in_proj(xs, nw, mods, w, tiles_per_batch, ctx_tiles):
    m, d = xs.shape
    n = w.shape[1]
    row = functools.partial(_mod_row, tiles_per_batch=tiles_per_batch, ctx_tiles=ctx_tiles)
    return pl.pallas_call(
        _in_proj_kernel,
        grid=(m // TM,),
        in_specs=[
            pl.BlockSpec((TM, d), lambda i: (i, 0)),
            pl.BlockSpec((1, d), lambda i: (0, 0)),
            pl.BlockSpec((None, 1, d), lambda i: (row(i), 0, 1)),
            pl.BlockSpec((None, 1, d), lambda i: (row(i), 0, 0)),
            pl.BlockSpec((d, n), lambda i: (0, 0)),
        ],
        out_specs=pl.BlockSpec((TM, n), lambda i: (i, 0)),
        out_shape=jax.ShapeDtypeStruct((m, n), F32),
        compiler_params=_params("parallel"),
        name="in_proj",
    )(xs, nw.reshape(1, d), mods, mods, w)


def _cumsum_rows(x, reverse):
    n = x.shape[0]
    row = lax.broadcasted_iota(jnp.int32, x.shape, 0)
    s = 1
    while s < n:
        if reverse:
            x = x + jnp.where(row < n - s, pltpu.roll(x, n - s, 0), 0.0)
        else:
            x = x + jnp.where(row >= s, pltpu.roll(x, s, 0), 0.0)
        s *= 2
    return x


def _bcast_row(x, r, rows):
    return jnp.broadcast_to(x[r:r + 1, :], (rows, x.shape[1]))


def _hg_chunk_head(qh, kh, vh, cum, st, reverse, masks, ones_rhs):
    c = qh.shape[0]
    last = cum[0:1, :] if reverse else cum[c - 1:c, :]
    o = _dot_nt((qh * jnp.exp(cum)).astype(BF16), st.astype(BF16))
    kd = kh * jnp.exp(last - cum)
    st_new = st * jnp.exp(last) + _dot_tn(vh.astype(BF16), kd.astype(BF16))

    a = jnp.zeros((c, c), F32)
    half = c // 2
    li = 0
    while half >= HG_SUB:
        blk = 2 * half
        refs = []
        for b0 in range(0, c, blk):
            r = b0 + half if reverse else b0 + half - 1
            refs.append(_bcast_row(cum, r, blk))
        ref = refs[0] if len(refs) == 1 else jnp.concatenate(refs, axis=0)
        qs = qh * jnp.exp(jnp.minimum(cum - ref, 0.0))
        ks = kh * jnp.exp(jnp.minimum(ref - cum, 0.0))
        a = a + jnp.where(masks[li], _dot_nt(qs.astype(BF16), ks.astype(BF16)), 0.0)
        half //= 2
        li += 1

    rsub = lax.broadcasted_iota(jnp.int32, (HG_SUB, LANES), 0)
    pieces = []
    for g in range(c // HG_SUB):
        r0 = g * HG_SUB
        qg = qh[r0:r0 + HG_SUB, :]
        cg = cum[r0:r0 + HG_SUB, :]
        for j in range(HG_SUB):
            cs = _bcast_row(cum, r0 + j, HG_SUB)
            ks = _bcast_row(kh, r0 + j, HG_SUB)
            p = qg * ks * jnp.exp(jnp.minimum(cg - cs, 0.0))
            valid = (rsub <= j) if reverse else (rsub >= j)
            pieces.append(jnp.where(valid, p, 0.0))
    rs = _dot(jnp.concatenate(pieces, axis=0).astype(BF16), ones_rhs)
    lane = lax.broadcasted_iota(jnp.int32, (HG_SUB, c), 1)
    diag = []
    for g in range(c // HG_SUB):
        acc = jnp.zeros((HG_SUB, c), F32)
        for j in range(HG_SUB):
            idx = g * HG_SUB + j
            acc = jnp.where(lane == idx, rs[idx * HG_SUB:(idx + 1) * HG_SUB, :], acc)
        diag.append(acc)
    a = a + jnp.concatenate(diag, axis=0)
    o = o + _dot(a.astype(BF16), vh.astype(BF16))
    return o, st_new


def _hg_kernel(*refs, reverse, final, rows):
    if final:
        q_ref, f_ref, v_ref, la_ref, lb_ref, g_ref, op_ref, nw_ref, o_ref, s_ref = refs
    else:
        q_ref, f_ref, v_ref, la_ref, lb_ref, o_ref, s_ref = refs

    @pl.when(pl.program_id(1) == 0)
    def _():
        s_ref[...] = jnp.zeros_like(s_ref)

    c = HG_CHUNK
    ti = lax.broadcasted_iota(jnp.int32, (c, c), 0)
    si = lax.broadcasted_iota(jnp.int32, (c, c), 1)
    masks = []
    half = c // 2
    while half >= HG_SUB:
        blk = 2 * half
        same = (ti & -blk) == (si & -blk)
        if reverse:
            m = same & ((ti & half) == 0) & ((si & half) != 0)
        else:
            m = same & ((ti & half) != 0) & ((si & half) == 0)
        masks.append(m)
        half //= 2
    ones_rhs = jnp.ones((LANES, c), BF16)

    n_ch = rows // c
    order = range(n_ch - 1, -1, -1) if reverse else range(n_ch)
    for ci in order:
        r0 = ci * c
        z = f_ref[r0:r0 + c, :]
        ls = jnp.minimum(z, 0.0) - jnp.log1p(jnp.exp(-jnp.abs(z)))
        bt = lb_ref[...] + ls
        at = jnp.broadcast_to(la_ref[...], bt.shape)
        logf = jnp.maximum(at, bt) + jnp.log1p(jnp.exp(-jnp.abs(at - bt)))
        kf = 1.0 - jnp.exp(logf)
        qf = _silu(q_ref[r0:r0 + c, :])
        vv = v_ref[r0:r0 + c, :]
        cum = _cumsum_rows(logf, reverse)
        outs = []
        for hh in range(HG_HEADS):
            sl = slice(hh * HG_DK, (hh + 1) * HG_DK)
            o, st_new = _hg_chunk_head(qf[:, sl], kf[:, sl], vv[:, sl], cum[:, sl], s_ref[hh],
                                       reverse, masks, ones_rhs)
            s_ref[hh] = st_new
            if final:
                tot = o + op_ref[r0:r0 + c, sl]
                ms = jnp.mean(tot * tot, axis=-1, keepdims=True)
                o = tot * lax.rsqrt(ms + EPS) * nw_ref[:, sl]
            outs.append(o)
        o_all = jnp.concatenate(outs, axis=1)
        if final:
            o_all = o_all * _silu(g_ref[r0:r0 + c, :])
        o_ref[r0:r0 + c, :] = o_all.astype(o_ref.dtype)


def _hg_direction(u, la, lb, batch, n_ctx, reverse, o_prev=None, nw=None):
    m = u.shape[0]
    t = m // batch
    nblk = t // TM
    ncb = n_ctx // TM
    final = o_prev is not None
    cb = OFF_HG // HG_WIDTH

    def blk(j):
        if not reverse:
            return j
        return jnp.where(j < ncb, ncb - 1 - j, nblk - 1 - (j - ncb))

    def tok(col):
        return pl.BlockSpec((TM, HG_WIDTH), lambda b, j: (b * nblk + blk(j), col))

    vec = pl.BlockSpec((1, HG_WIDTH), lambda b, j: (0, 0))
    f_col = cb + 2 if reverse else cb + 1
    in_specs = [tok(cb), tok(f_col), tok(cb + 3), vec, vec]
    args = [u, u, u, la.reshape(1, -1), lb.reshape(1, -1)]
    if final:
        in_specs += [tok(cb + 4), tok(0), vec]
        args += [u, o_prev, nw.reshape(1, -1)]
    return pl.pallas_call(
        functools.partial(_hg_kernel, reverse=reverse, final=final, rows=TM),
        grid=(batch, nblk),
        in_specs=in_specs,
        out_specs=tok(0),
        out_shape=jax.ShapeDtypeStruct((m, HG_WIDTH), BF16 if final else F32),
        scratch_shapes=[pltpu.VMEM((HG_HEADS, HG_DK, HG_DK), F32)],
        compiler_params=_params("parallel", "arbitrary"),
        name="hgrn2_bwd" if reverse else "hgrn2_fwd",
    )(*args)


def _swap_halves(x):
    lane = lax.broadcasted_iota(jnp.int32, x.shape, 1)
    return jnp.where((lane & 32) == 0, pltpu.roll(x, LANES - 32, 1), pltpu.roll(x, 32, 1))


def _head_norm_rope(x, w, cos, sin, gmat):
    x2 = x * x
    hi = x2.astype(BF16)
    lo = (x2 - hi.astype(F32)).astype(BF16)
    ms = (_dot(hi, gmat) + _dot(lo, gmat)) * (1.0 / HEAD_DIM)
    xn = x * lax.rsqrt(ms + EPS) * w
    return xn * cos + _swap_halves(xn) * sin


def _att_prep_kernel(q_ref, k_ref, v_ref, cos_ref, sin_ref, qw_ref, kw_ref, g_ref,
                     qo_ref, k0_ref, k1_ref, v0_ref, v1_ref):
    cos, sin, gmat = cos_ref[...], sin_ref[...], g_ref[...]
    qscale = (HEAD_DIM ** -0.5) * float(np.log2(np.e))
    for j in range(ATT_WIDTH // LANES):
        sl = slice(j * LANES, (j + 1) * LANES)
        qr = _head_norm_rope(q_ref[:, sl], qw_ref[...], cos, sin, gmat)
        qo_ref[:, sl] = (qr * qscale).astype(BF16)
    kr = _head_norm_rope(k_ref[...], kw_ref[...], cos, sin, gmat)
    lane = lax.broadcasted_iota(jnp.int32, kr.shape, 1)
    k0_ref[...] = jnp.where(lane < HEAD_DIM, kr, 0.0).astype(BF16)
    k1_ref[...] = jnp.where(lane >= HEAD_DIM, kr, 0.0).astype(BF16)
    v = v_ref[...]
    v0_ref[...] = jnp.where(lane < HEAD_DIM, v, 1.0).astype(BF16)
    v1_ref[...] = jnp.where(lane >= HEAD_DIM, v, 1.0).astype(BF16)


def _att_prep(u, cos, sin, qw, kw, gmat, batch):
    m = u.shape[0]
    tpb = m // batch // TM
    kvw = ATT_KV_HEADS * HEAD_DIM
    tab = pl.BlockSpec((TM, LANES), lambda i: (i % tpb, 0))
    vec = pl.BlockSpec((1, LANES), lambda i: (0, 0))
    kv_out = pl.BlockSpec((TM, kvw), lambda i: (i, 0))
    return pl.pallas_call(
        _att_prep_kernel,
        grid=(m // TM,),
        in_specs=[
            pl.BlockSpec((TM, ATT_WIDTH), lambda i: (i, OFF_ATT_Q // ATT_WIDTH)),
            pl.BlockSpec((TM, kvw), lambda i: (i, OFF_ATT_K // kvw)),
            pl.BlockSpec((TM, kvw), lambda i: (i, OFF_ATT_V // kvw)),
            tab, tab, vec, vec,
            pl.BlockSpec((LANES, LANES), lambda i: (0, 0)),
        ],
        out_specs=[pl.BlockSpec((TM, ATT_WIDTH), lambda i: (i, 0)), kv_out, kv_out, kv_out, kv_out],
        out_shape=[jax.ShapeDtypeStruct((m, ATT_WIDTH), BF16)] + [jax.ShapeDtypeStruct((m, kvw), BF16)] * 4,
        compiler_params=_params("parallel"),
        name="att_prep",
    )(u, u, u, cos, sin, qw, kw, gmat)


def _att_kernel(bound_ref, q_ref, k0_ref, k1_ref, v0_ref, v1_ref, o_ref, *, n_ctx, n_all, tq):
    bound = bound_ref[0, 0]

    def attend(r0, rows, n_keys, use_bound):
        q = q_ref[pl.ds(r0, rows), :]

        def head(k_ref, v_ref):
            s = _dot_nt(q, k_ref[0:n_keys, :])
            shift = bound if use_bound else jnp.max(s, axis=-1, keepdims=True)
            return _dot(jnp.exp2(s - shift).astype(BF16), v_ref[0:n_keys, :])

        acc0, acc1 = head(k0_ref, v0_ref), head(k1_ref, v1_ref)
        lane = lax.broadcasted_iota(jnp.int32, acc0.shape, 1)
        out = jnp.where(lane < HEAD_DIM, acc0 / pltpu.roll(acc0, HEAD_DIM, 1),
                        acc1 / pltpu.roll(acc1, HEAD_DIM, 1))
        o_ref[pl.ds(r0, rows), :] = out.astype(o_ref.dtype)

    def all_rows(use_bound):
        if n_ctx:
            attend(0, n_ctx, n_ctx, use_bound)

        def lat_chunk(i, _):
            attend(pl.multiple_of(n_ctx + i * tq, TM), tq, n_all, use_bound)
            return 0

        lax.fori_loop(0, (n_all - n_ctx) // tq, lat_chunk, 0)

    safe = bound <= ATT_SAFE_BOUND

    @pl.when(safe)
    def _():
        all_rows(True)

    @pl.when(jnp.logical_not(safe))
    def _():
        all_rows(False)


def _attention(bound, qr, k0, k1, v0, v1, batch, n_ctx):
    m = qr.shape[0]
    t = m // batch
    kvw = ATT_KV_HEADS * HEAD_DIM
    tq = ATT_TQ if (t - n_ctx) % ATT_TQ == 0 else TM
    qspec = pl.BlockSpec((t, LANES), lambda b, j: (b, j))
    kvspec = pl.BlockSpec((t, kvw), lambda b, j: (b, 0))
    return pl.pallas_call(
        functools.partial(_att_kernel, n_ctx=n_ctx, n_all=t, tq=tq),
        grid=(batch, ATT_WIDTH // LANES),
        in_specs=[pl.BlockSpec(memory_space=pltpu.SMEM), qspec, kvspec, kvspec, kvspec, kvspec],
        out_specs=qspec,
        out_shape=jax.ShapeDtypeStruct((m, ATT_WIDTH), BF16),
        compiler_params=_params("parallel", "parallel"),
        name="attention",
    )(bound, qr, k0, k1, v0, v1)


def _scan_rows(a, u, reverse):
    n = a.shape[0]
    row = lax.broadcasted_iota(jnp.int32, a.shape, 0)
    s = 1
    while s < n:
        if reverse:
            keep = row < n - s
            ash, ush = pltpu.roll(a, n - s, 0), pltpu.roll(u, n - s, 0)
        else:
            keep = row >= s
            ash, ush = pltpu.roll(a, s, 0), pltpu.roll(u, s, 0)
        u = u + a * jnp.where(keep, ush, 0.0)
        a = a * jnp.where(keep, ash, 1.0)
        s *= 2
    return a, u


def _lru_kernel(x_ref, g_ref, cw_ref, cb_ref, w_ref, b_ref, lam_ref, o_ref, xp_ref, xc_ref, h_ref,
                *, n_ctx, n_tok):
    tt = TM
    pad = SUBLANES
    n_lat = n_tok - n_ctx
    lat0 = n_ctx + 2 * pad
    zeros = jnp.zeros((pad, LANES), F32)
    xp_ref[0:pad, :] = zeros
    xp_ref[pad + n_ctx:lat0, :] = zeros
    xp_ref[lat0 + n_lat:lat0 + n_lat + pad, :] = zeros
    xp_ref[pad:pad + n_ctx, :] = x_ref[0:n_ctx, :]
    xp_ref[lat0:lat0 + n_lat, :] = x_ref[n_ctx:n_tok, :]

    n_tiles = n_tok // tt
    ctx_tiles = n_ctx // tt

    def conv_tile(i, _):
        r0 = pl.multiple_of(i * tt, tt)
        p0 = pl.multiple_of(jnp.where(i < ctx_tiles, r0, r0 + pad), pad)
        ext = xp_ref[pl.ds(p0, tt + 2 * pad), :]
        acc = cb_ref[...] + cw_ref[0:1, :] * ext[pad - 2:pad - 2 + tt, :]
        for j in range(1, CONV_W):
            acc = acc + cw_ref[j:j + 1, :] * ext[pad - 2 + j:pad - 2 + j + tt, :]
        xc_ref[pl.ds(r0, tt), :] = acc
        return 0

    lax.fori_loop(0, n_tiles, conv_tile, 0)

    lam = lam_ref[...]
    sp = jnp.maximum(-lam, 0.0) + jnp.log1p(jnp.exp(-jnp.abs(lam)))

    def coeffs(r0, d):
        xc = xc_ref[pl.ds(r0, tt), :]
        pre = _dot(xc.astype(BF16), w_ref[:, d * 2 * LANES:(d + 1) * 2 * LANES]) \
            + b_ref[:, d * 2 * LANES:(d + 1) * 2 * LANES]
        r = _sigmoid(pre[:, :LANES])
        ig = _sigmoid(pre[:, LANES:])
        log_a = (-RG_C) * r * sp[d:d + 1, :]
        a = jnp.exp(log_a)
        return a, jnp.sqrt(1.0 - a * a) * (ig * xc)

    def fwd_tile(i, carry):
        r0 = pl.multiple_of(i * tt, tt)
        a, u = coeffs(r0, 0)
        a, u = _scan_rows(a, u, False)
        h = u + a * carry
        h_ref[pl.ds(r0, tt), :] = h
        return h[tt - 1:tt, :]

    lax.fori_loop(0, n_tiles, fwd_tile, jnp.zeros((1, LANES), F32))

    def bwd_tile(j, carry):
        i = jnp.where(j < ctx_tiles, ctx_tiles - 1 - j, n_tiles - 1 - (j - ctx_tiles))
        r0 = pl.multiple_of(i * tt, tt)
        a, u = coeffs(r0, 1)
        a, u = _scan_rows(a, u, True)
        h = u + a * carry
        tot = h + h_ref[pl.ds(r0, tt), :]
        o_ref[pl.ds(r0, tt), :] = (tot * jax.nn.gelu(g_ref[pl.ds(r0, tt), :])).astype(o_ref.dtype)
        return h[0:1, :]

    lax.fori_loop(0, n_tiles, bwd_tile, jnp.zeros((1, LANES), F32))


def _rglru(u, conv_w, conv_b, wcat, bcat, lam, batch, n_ctx):
    m = u.shape[0]
    t = m // batch
    ncg = LRU_WIDTH // LANES
    xcol = OFF_LRU // LANES
    return pl.pallas_call(
        functools.partial(_lru_kernel, n_ctx=n_ctx, n_tok=t),
        grid=(batch, ncg),
        in_specs=[
            pl.BlockSpec((t, LANES), lambda b, c: (b, xcol + c)),
            pl.BlockSpec((t, LANES), lambda b, c: (b, xcol + ncg + c)),
            pl.BlockSpec((CONV_W, LANES), lambda b, c: (0, c)),
            pl.BlockSpec((1, LANES), lambda b, c: (0, c)),
            pl.BlockSpec((None, LANES, 4 * LANES), lambda b, c: (c, 0, 0)),
            pl.BlockSpec((None, 1, 4 * LANES), lambda b, c: (c, 0, 0)),
            pl.BlockSpec((2, LANES), lambda b, c: (0, c)),
        ],
        out_specs=pl.BlockSpec((t, LANES), lambda b, c: (b, c)),
        out_shape=jax.ShapeDtypeStruct((m, LRU_WIDTH), BF16),
        scratch_shapes=[pltpu.VMEM((t + 3 * SUBLANES, LANES), F32),
                        pltpu.VMEM((t, LANES), F32),
                        pltpu.VMEM((t, LANES), F32)],
        compiler_params=_params("parallel", "parallel"),
        name="rglru",
    )(u, u, conv_w, conv_b.reshape(1, -1), wcat, bcat, lam)


def _merge_kernel(ya_ref, yb_ref, yc_ref, ml_ref, xs_ref, g1_ref, wa_ref, wb_ref, wc_ref, wo_ref, o_ref):
    d = D_MODEL
    acc = _sigmoid(ml_ref[:, 0:d]) * _dot(ya_ref[...], wa_ref[...])
    acc = acc + _sigmoid(ml_ref[:, d:2 * d]) * _dot(yb_ref[...], wb_ref[...])
    acc = acc + _sigmoid(ml_ref[:, 2 * d:3 * d]) * _dot(yc_ref[...], wc_ref[...])
    y = _dot(acc.astype(BF16), wo_ref[...])
    o_ref[...] = xs_ref[...] + g1_ref[...] * y


def _merge(ya, yb, yc, u, xs, mods, wa, wb, wc, wo, batch, n_ctx, drop_ctx):
    m, d = xs.shape
    tpb = m // batch // TM
    ctx_tiles = n_ctx // TM
    if drop_ctx:
        out_tpb = tpb - ctx_tiles
        src = lambda i: (i // out_tpb) * tpb + ctx_tiles + i % out_tpb
        row = lambda i: i // out_tpb
        m_out = batch * out_tpb * TM
    else:
        src = lambda i: i
        row = functools.partial(_mod_row, tiles_per_batch=tpb, ctx_tiles=ctx_tiles)
        m_out = m
    br = pl.BlockSpec((TM, HG_WIDTH), lambda i: (src(i), 0))
    full = lambda a: pl.BlockSpec(a.shape, lambda i: (0, 0))
    return pl.pallas_call(
        _merge_kernel,
        grid=(m_out // TM,),
        in_specs=[br, br, br,
                  pl.BlockSpec((TM, 3 * d), lambda i: (src(i), OFF_MERGE)),
                  pl.BlockSpec((TM, d), lambda i: (src(i), 0)),
                  pl.BlockSpec((None, 1, d), lambda i: (row(i), 0, 2)),
                  full(wa), full(wb), full(wc), full(wo)],
        out_specs=pl.BlockSpec((TM, d), lambda i: (i, 0)),
        out_shape=jax.ShapeDtypeStruct((m_out, d), F32),
        compiler_params=_params("parallel"),
        name="merge_out",
    )(ya, yb, yc, u, xs, mods, wa, wb, wc, wo)


def _swiglu_acc(hb, wg_ref, wu_ref, wd_ref):
    acc = jnp.zeros((hb.shape[0], wd_ref.shape[1]), F32)
    for c0 in range(0, D_FF, FF_CHUNK):
        a = _dot(hb, wg_ref[:, c0:c0 + FF_CHUNK])
        b = _dot(hb, wu_ref[:, c0:c0 + FF_CHUNK])
        acc = acc + _dot((_silu(a) * b).astype(BF16), wd_ref[c0:c0 + FF_CHUNK, :])
    return acc


def _final_norm(y, fw):
    ms = jnp.mean(y * y, axis=-1, keepdims=True)
    return y * lax.rsqrt(ms + EPS) * fw


def _ffn_kernel(x_ref, nw_ref, sc_ref, sh_ref, g2_ref, wg_ref, wu_ref, wd_ref, fw_ref, o_ref, *, final):
    x = x_ref[...]
    hb = _norm_mod(x, nw_ref[...], sc_ref[...], sh_ref[...]).astype(BF16)
    y = x + g2_ref[...] * _swiglu_acc(hb, wg_ref, wu_ref, wd_ref)
    o_ref[...] = _final_norm(y, fw_ref[...]) if final else y


def _mod_specs(row, d):
    return [pl.BlockSpec((None, 1, d), lambda i, *_: (row(i), 0, 4)),
            pl.BlockSpec((None, 1, d), lambda i, *_: (row(i), 0, 3)),
            pl.BlockSpec((None, 1, d), lambda i, *_: (row(i), 0, 5))]


def _ffn(xs, nw, mods, wg, wu, wd, fw, tiles_per_batch, ctx_tiles, final):
    m, d = xs.shape
    row = functools.partial(_mod_row, tiles_per_batch=tiles_per_batch, ctx_tiles=ctx_tiles)
    full = lambda a: pl.BlockSpec(a.shape, lambda i: (0, 0))
    vec = pl.BlockSpec((1, d), lambda i: (0, 0))
    return pl.pallas_call(
        functools.partial(_ffn_kernel, final=final),
        grid=(m // TM,),
        in_specs=[pl.BlockSpec((TM, d), lambda i: (i, 0)), vec] + _mod_specs(row, d)
                 + [full(wg), full(wu), full(wd), vec],
        out_specs=pl.BlockSpec((TM, d), lambda i: (i, 0)),
        out_shape=jax.ShapeDtypeStruct((m, d), F32),
        compiler_params=_params("parallel"),
        name="ffn_dense",
    )(xs, nw.reshape(1, d), mods, mods, mods, wg, wu, wd, fw.reshape(1, d))


def _top2(logits):
    lane = lax.broadcasted_iota(jnp.int32, logits.shape, 1)
    m1 = jnp.max(logits, axis=-1, keepdims=True)
    i1 = jnp.min(jnp.where(logits == m1, lane, LANES), axis=-1, keepdims=True)
    rest = jnp.where(lane == i1, -jnp.inf, logits)
    m2 = jnp.max(rest, axis=-1, keepdims=True)
    i2 = jnp.min(jnp.where(rest == m2, lane, LANES), axis=-1, keepdims=True)
    e2 = jnp.exp(m2 - m1)
    w1 = 1.0 / (1.0 + e2)
    return i1, i2, w1, e2 * w1


def _router_kernel(x_ref, nw_ref, sc_ref, sh_ref, rw_ref, meta_ref, wts_ref, cnt_ref, base_ref):
    @pl.when(pl.program_id(0) == 0)
    def _():
        base_ref[...] = jnp.zeros_like(base_ref)

    h = _norm_mod(x_ref[...], nw_ref[...], sc_ref[...], sh_ref[...])
    logits = jnp.dot(h, rw_ref[...], preferred_element_type=F32, precision=lax.Precision.HIGHEST)
    lane = lax.broadcasted_iota(jnp.int32, logits.shape, 1)
    i1, i2, w1, w2 = _top2(jnp.where(lane < N_EXPERTS, logits, -jnp.inf))
    hit = jnp.where((lane == i1) | (lane == i2), 1.0, 0.0)
    tm = hit.shape[0]
    tri = lax.broadcasted_iota(jnp.int32, (tm, tm), 0) > lax.broadcasted_iota(jnp.int32, (tm, tm), 1)
    before = _dot(jnp.where(tri, 1.0, 0.0).astype(BF16), hit.astype(BF16)) + base_ref[...]
    r1 = jnp.sum(jnp.where(lane == i1, before, 0.0), axis=-1, keepdims=True).astype(jnp.int32)
    r2 = jnp.sum(jnp.where(lane == i2, before, 0.0), axis=-1, keepdims=True).astype(jnp.int32)
    base_ref[...] += jnp.sum(hit, axis=0, keepdims=True)
    cnt_ref[...] = base_ref[...]
    meta_ref[...] = jnp.where(lane == 0, i1, jnp.where(lane == 1, i2, jnp.where(lane == 2, r1,
                              jnp.where(lane == 3, r2, 0))))
    wts_ref[...] = jnp.where(lane == 0, w1, jnp.where(lane == 1, w2, 0.0))


def _router(xs, nw, mods, rw, row):
    m, d = xs.shape
    vec = pl.BlockSpec((1, d), lambda i: (0, 0))
    tok = pl.BlockSpec((TM, LANES), lambda i: (i, 0))
    return pl.pallas_call(
        _router_kernel,
        grid=(m // TM,),
        in_specs=[pl.BlockSpec((TM, d), lambda i: (i, 0)), vec] + _mod_specs(row, d)[:2]
                 + [pl.BlockSpec((d, LANES), lambda i: (0, 0))],
        out_specs=[tok, tok, pl.BlockSpec((1, LANES), lambda i: (0, 0))],
        out_shape=[jax.ShapeDtypeStruct((m, LANES), jnp.int32), jax.ShapeDtypeStruct((m, LANES), F32),
                   jax.ShapeDtypeStruct((1, LANES), F32)],
        scratch_shapes=[pltpu.VMEM((1, LANES), F32)],
        compiler_params=_params("arbitrary"),
        name="moe_router",
    )(xs, nw.reshape(1, d), mods, mods, rw)


def _row_copy(src_ref, s, dst_ref, t, sem):
    return pltpu.make_async_copy(src_ref.at[pl.ds(s, 1)], dst_ref.at[pl.ds(t, 1)], sem)


def _dispatch_kernel(pos_ref, x_ref, nw_ref, sc_ref, sh_ref, xin_ref, xbuf_ref, h_ref, sem):
    del xin_ref
    h_ref[...] = _norm_mod(x_ref[...], nw_ref[...], sc_ref[...], sh_ref[...])
    tm = h_ref.shape[0]

    def issue(r, _):
        _row_copy(h_ref, r, xbuf_ref, pos_ref[0, 2 * r], sem).start()
        _row_copy(h_ref, r, xbuf_ref, pos_ref[0, 2 * r + 1], sem).start()
        return 0

    lax.fori_loop(0, tm, issue, 0)

    def drain(r, _):
        _row_copy(h_ref, 0, xbuf_ref, 0, sem).wait()
        _row_copy(h_ref, 0, xbuf_ref, 0, sem).wait()
        return 0

    lax.fori_loop(0, tm, drain, 0)


def _dispatch(xs, nw, mods, pos, row, n_rows):
    m, d = xs.shape
    vec = pl.BlockSpec((1, d), lambda i: (0, 0))
    zeros = jnp.zeros((n_rows, d), F32)
    return pl.pallas_call(
        _dispatch_kernel,
        grid=(m // TM,),
        in_specs=[pl.BlockSpec((None, 1, 2 * TM), lambda i: (i, 0, 0), memory_space=pltpu.SMEM),
                  pl.BlockSpec((TM, d), lambda i: (i, 0)), vec] + _mod_specs(row, d)[:2]
                 + [pl.BlockSpec(memory_space=pl.ANY)],
        out_specs=pl.BlockSpec(memory_space=pl.ANY),
        out_shape=jax.ShapeDtypeStruct((n_rows, d), F32),
        scratch_shapes=[pltpu.VMEM((TM, d), F32), pltpu.SemaphoreType.DMA(())],
        input_output_aliases={5: 0},
        compiler_params=_params("arbitrary"),
        name="moe_dispatch",
    )(pos, xs, nw.reshape(1, d), mods, mods, zeros)


def _expert_ffn_kernel(te_ref, nu_ref, x_ref, wg_ref, wu_ref, wd_ref, y_ref):
    del te_ref
    used = pl.program_id(0) < nu_ref[0]

    @pl.when(used)
    def _():
        y_ref[...] = _swiglu_acc(x_ref[...].astype(BF16), wg_ref, wu_ref, wd_ref)

    @pl.when(jnp.logical_not(used))
    def _():
        y_ref[...] = jnp.zeros_like(y_ref)


def _expert_ffn(xbuf, tile_expert, n_used, wg, wu, wd):
    n_rows, d = xbuf.shape
    wspec = lambda shape: pl.BlockSpec((None,) + shape, lambda i, te, nu: (te[i], 0, 0))
    return pl.pallas_call(
        _expert_ffn_kernel,
        grid_spec=pltpu.PrefetchScalarGridSpec(
            num_scalar_prefetch=2,
            grid=(n_rows // TM,),
            in_specs=[pl.BlockSpec((TM, d), lambda i, te, nu: (i, 0)),
                      wspec((d, D_FF)), wspec((d, D_FF)), wspec((D_FF, d))],
            out_specs=pl.BlockSpec((TM, d), lambda i, te, nu: (i, 0)),
        ),
        out_shape=jax.ShapeDtypeStruct((n_rows, d), F32),
        compiler_params=_params("arbitrary"),
        name="moe_expert_ffn",
    )(tile_expert, n_used, xbuf, wg, wu, wd)


def _combine_kernel(pos_ref, x_ref, wts_ref, g2_ref, fw_ref, y_ref, o_ref, buf_ref, sem, *, final):
    tm = x_ref.shape[0]

    def issue(r, _):
        _row_copy(y_ref, pos_ref[0, 2 * r], buf_ref.at[0], r, sem).start()
        _row_copy(y_ref, pos_ref[0, 2 * r + 1], buf_ref.at[1], r, sem).start()
        return 0

    lax.fori_loop(0, tm, issue, 0)

    def drain(r, _):
        _row_copy(y_ref, 0, buf_ref.at[0], 0, sem).wait()
        _row_copy(y_ref, 0, buf_ref.at[0], 0, sem).wait()
        return 0

    lax.fori_loop(0, tm, drain, 0)
    w = wts_ref[...]
    f = w[:, 0:1] * buf_ref[0] + w[:, 1:2] * buf_ref[1]
    y = x_ref[...] + g2_ref[...] * f
    o_ref[...] = _final_norm(y, fw_ref[...]) if final else y


def _combine(xs, wts, mods, fw, ybuf, pos, row, final):
    m, d = xs.shape
    vec = pl.BlockSpec((1, d), lambda i: (0, 0))
    return pl.pallas_call(
        functools.partial(_combine_kernel, final=final),
        grid=(m // TM,),
        in_specs=[pl.BlockSpec((None, 1, 2 * TM), lambda i: (i, 0, 0), memory_space=pltpu.SMEM),
                  pl.BlockSpec((TM, d), lambda i: (i, 0)),
                  pl.BlockSpec((TM, LANES), lambda i: (i, 0)),
                  _mod_specs(row, d)[2], vec,
                  pl.BlockSpec(memory_space=pl.ANY)],
        out_specs=pl.BlockSpec((TM, d), lambda i: (i, 0)),
        out_shape=jax.ShapeDtypeStruct((m, d), F32),
        scratch_shapes=[pltpu.VMEM((2, TM, d), F32), pltpu.SemaphoreType.DMA(())],
        compiler_params=_params("arbitrary"),
        name="moe_combine",
    )(pos, xs, wts, mods, fw.reshape(1, d), ybuf)


def _moe(xs, nw, mods, rw, wg, wu, wd, fw, tiles_per_batch, ctx_tiles, final):
    m, d = xs.shape
    row = functools.partial(_mod_row, tiles_per_batch=tiles_per_batch, ctx_tiles=ctx_tiles)
    meta, wts, cnt = _router(xs, nw, mods, rw, row)
    counts = cnt[0, :N_EXPERTS].astype(jnp.int32)
    padded = (counts + TM - 1) // TM * TM
    ends = jnp.cumsum(padded)
    starts = ends - padded
    pos = jnp.stack([starts[meta[:, 0]] + meta[:, 2], starts[meta[:, 1]] + meta[:, 3]], axis=-1)
    pos = pos.reshape(m // TM, 1, 2 * TM)
    n_rows = 2 * m + N_EXPERTS * TM
    n_used = (ends[-1] // TM).astype(jnp.int32)
    tile_start = jnp.arange(n_rows // TM, dtype=jnp.int32) * TM
    tile_expert = jnp.searchsorted(ends, jnp.minimum(tile_start, ends[-1] - 1), side="right").astype(jnp.int32)
    xbuf = _dispatch(xs, nw, mods, pos, row, n_rows)
    ybuf = _expert_ffn(xbuf, tile_expert, n_used.reshape(1), wg, wu, wd)
    return _combine(xs, wts, mods, fw, ybuf, pos, row, final)


def _column_perm():
    sizes = (512, 512, 512, 512, 512, 512, 128, 128, 512, 512, 3 * D_MODEL)
    starts = np.concatenate([[0], np.cumsum(sizes)[:-1]])
    a_q, a_ff, a_fb, a_v, a_g, b_q, b_k, b_v, c_x, c_g, mrg = [
        np.arange(s, s + n) for s, n in zip(starts, sizes)]
    half = np.concatenate([np.arange(0, HEAD_DIM, 2), np.arange(1, HEAD_DIM, 2)])
    group = ATT_HEADS // ATT_KV_HEADS
    q_cols = []
    for j in range(group):
        q_cols += [b_q[j * HEAD_DIM + half], b_q[(group + j) * HEAD_DIM + half]]
    k_cols = [b_k[h * HEAD_DIM + half] for h in range(ATT_KV_HEADS)]
    return np.concatenate([mrg, a_q, a_ff, a_fb, a_v, a_g, c_x, c_g] + q_cols + k_cols + [b_v])


def _att_out_rows():
    group = ATT_HEADS // ATT_KV_HEADS
    idx = []
    for j in range(group):
        idx += [np.arange(j * HEAD_DIM, (j + 1) * HEAD_DIM),
                np.arange((group + j) * HEAD_DIM, (group + j + 1) * HEAD_DIM)]
    return np.concatenate(idx)


def _rope_tables(n_ctx, n_lat):
    rows = n_lat // GRID_W
    row = jnp.repeat(jnp.arange(rows, dtype=F32), GRID_W)
    col = jnp.tile(jnp.arange(GRID_W, dtype=F32), rows)
    pairs = HEAD_DIM // 4
    freqs = ROPE_THETA ** (-jnp.arange(pairs, dtype=F32) / pairs)
    ang = jnp.concatenate([row[:, None] * freqs, col[:, None] * freqs], axis=-1)
    cos, sin = jnp.cos(ang), jnp.sin(ang)
    cos = jnp.concatenate([jnp.ones((n_ctx, HEAD_DIM // 2), F32), cos], axis=0)
    sin = jnp.concatenate([jnp.zeros((n_ctx, HEAD_DIM // 2), F32), sin], axis=0)
    reps = LANES // HEAD_DIM
    return (jnp.tile(jnp.concatenate([cos, cos], axis=-1), (1, reps)),
            jnp.tile(jnp.concatenate([-sin, sin], axis=-1), (1, reps)))


def _lru_weights(wa, wx, ba, bx):
    ncg = LRU_WIDTH // LANES

    def dense(w):
        w = w.reshape(ncg, 2, LRU_BLOCK_W, LRU_BLOCK_W)
        out = jnp.zeros((ncg, LANES, LANES), w.dtype)
        out = out.at[:, :LRU_BLOCK_W, :LRU_BLOCK_W].set(w[:, 0])
        return out.at[:, LRU_BLOCK_W:, LRU_BLOCK_W:].set(w[:, 1])

    wcat = jnp.concatenate([dense(wa[0]), dense(wx[0]), dense(wa[1]), dense(wx[1])], axis=-1).astype(BF16)
    bcat = jnp.concatenate([b.reshape(ncg, 1, LANES) for b in (ba[0], bx[0], ba[1], bx[1])], axis=-1)
    return wcat, bcat


def kernel(x, c, ctx, c_ctx, ada_w, ada_b, mix_norm_w, ffn_norm_w, w_in, hg_lb_logits, hg_norm_w, q_norm_w, k_norm_w, lru_conv_w, lru_conv_b, lru_wa, lru_ba, lru_wx, lru_bx, lru_lambda, w_br_a, w_br_b, w_br_c, w_out, ffn_w_gate, ffn_w_up, ffn_w_down, router_w, moe_w_gate, moe_w_up, moe_w_down, final_norm_w):
    batch, n_lat, d = x.shape
    n_ctx = ctx.shape[1]
    t = n_ctx + n_lat
    depth = ada_w.shape[0]
    assert n_ctx % TM == 0 and n_lat % TM == 0 and batch <= 4

    cvec = jnp.zeros((SUBLANES, d), F32).at[:batch].set(c).at[4].set(c_ctx)
    mods_all = _ada_mods(cvec, ada_w, ada_b).reshape(depth, SUBLANES, 1, 6 * d)

    p = jax.nn.softmax(hg_lb_logits.astype(F32), axis=0)
    cum = jnp.cumsum(p, axis=0)
    lbs = cum - cum[:1]
    log_lb, log_1m_lb = jnp.log(lbs), jnp.log1p(-lbs)

    cos, sin = _rope_tables(n_ctx, n_lat)
    half = np.concatenate([np.arange(0, HEAD_DIM, 2), np.arange(1, HEAD_DIM, 2)])
    gidx = np.arange(LANES) // HEAD_DIM
    gmat = jnp.asarray(gidx[:, None] == gidx[None, :], BF16)
    col_perm = _column_perm()
    att_rows = _att_out_rows()

    xs = jnp.concatenate([ctx.astype(x.dtype), x], axis=1).reshape(batch * t, d)
    tpb, ctx_tiles = t // TM, n_ctx // TM
    for i in range(depth):
        last = i == depth - 1
        mods = mods_all[i]
        w_i = w_in[i][:, col_perm].astype(BF16)
        u = _in_proj(xs, mix_norm_w[i], mods, w_i, tpb, ctx_tiles)

        o_f = _hg_direction(u, log_lb[i, 0], log_1m_lb[i, 0], batch, n_ctx, False)
        y_a = _hg_direction(u, log_lb[i, 1], log_1m_lb[i, 1], batch, n_ctx, True, o_f, hg_norm_w[i])

        reps = LANES // HEAD_DIM
        qw = jnp.tile(q_norm_w[i][half], reps).reshape(1, LANES)
        kw = jnp.tile(k_norm_w[i][half], reps).reshape(1, LANES)
        qr, k0, k1, v0, v1 = _att_prep(u, cos, sin, qw, kw, gmat, batch)
        bound = (1.02 * HEAD_DIM ** 0.5 * float(np.log2(np.e))
                 * jnp.max(jnp.abs(q_norm_w[i])) * jnp.max(jnp.abs(k_norm_w[i]))).reshape(1, 1)
        y_b = _attention(bound, qr, k0, k1, v0, v1, batch, n_ctx)

        wcat, bcat = _lru_weights(lru_wa[i], lru_wx[i], lru_ba[i], lru_bx[i])
        y_c = _rglru(u, lru_conv_w[i], lru_conv_b[i], wcat, bcat, lru_lambda[i], batch, n_ctx)

        xs = _merge(y_a, y_b, y_c, u, xs, mods, w_br_a[i].astype(BF16), w_br_b[i][att_rows].astype(BF16),
                    w_br_c[i].astype(BF16), w_out[i].astype(BF16), batch, n_ctx, last)
        if last:
            tpb, ctx_tiles = n_lat // TM, 0
        if i % 2 == 0:
            j = i // 2
            xs = _ffn(xs, ffn_norm_w[i], mods, ffn_w_gate[j].astype(BF16), ffn_w_up[j].astype(BF16),
                      ffn_w_down[j].astype(BF16), final_norm_w, tpb, ctx_tiles, last)
        else:
            j = i // 2
            rw = jnp.zeros((d, LANES), F32).at[:, :N_EXPERTS].set(router_w[j])
            xs = _moe(xs, ffn_norm_w[i], mods, rw, moe_w_gate[j].astype(BF16), moe_w_up[j].astype(BF16),
                      moe_w_down[j].astype(BF16), final_norm_w, tpb, ctx_tiles, last)
    return xs.reshape(batch, n_lat, d)
```

```python
import functools

import numpy as np
import jax
import jax.numpy as jnp
from jax import lax
from jax.experimental import pallas as pl
from jax.experimental.pallas import tpu as pltpu

F32 = jnp.float32
BF16 = jnp.bfloat16

D_MODEL = 1024
DEPTH = 4
GRID_W = 64
HG_HEADS = 4
HG_DK = 128
HG_WIDTH = 512
ATT_HEADS = 8
ATT_KV_HEADS = 2
HEAD_DIM = 64
ATT_WIDTH = 512
ROPE_THETA = 10000.0
LRU_WIDTH = 512
LRU_BLOCK_W = 64
CONV_W = 4
RG_C = 8.0
D_FF = 2816
N_EXPERTS = 8
EPS = 1e-6

LANES = 128
SUBLANES = 8
TM = 256
HG_CHUNK = 64
HG_SUB = 8
ATT_TQ = 512
ATT_SAFE_BOUND = 60.0
FF_CHUNK = 1408
DMA_UNROLL = 8
VMEM_LIMIT = 56 * 1024 * 1024

OFF_MERGE = 0
OFF_HG = 3 * D_MODEL
OFF_LRU = OFF_HG + 5 * HG_WIDTH
OFF_ATT_Q = OFF_LRU + 2 * LRU_WIDTH
OFF_ATT_K = OFF_ATT_Q + ATT_WIDTH
OFF_ATT_V = OFF_ATT_K + ATT_KV_HEADS * HEAD_DIM
IN_DIM = OFF_ATT_V + ATT_KV_HEADS * HEAD_DIM


def _dot(a, b):
    return jnp.dot(a, b, preferred_element_type=F32)


def _dot_nt(a, b):
    return lax.dot_general(a, b, (((1,), (1,)), ((), ())), preferred_element_type=F32)


def _dot_tn(a, b):
    return lax.dot_general(a, b, (((0,), (0,)), ((), ())), preferred_element_type=F32)


def _sigmoid(x):
    return 1.0 / (1.0 + jnp.exp(-x))


def _silu(x):
    return x * _sigmoid(x)


def _params(*sem):
    return pltpu.CompilerParams(dimension_semantics=sem, vmem_limit_bytes=VMEM_LIMIT)


def _mod_row(i, tiles_per_batch, ctx_tiles):
    if ctx_tiles == 0:
        return i // tiles_per_batch
    return jnp.where(i % tiles_per_batch < ctx_tiles, 4, i // tiles_per_batch)


def _norm_mod(x, nw, sc, sh):
    ms = jnp.mean(x * x, axis=-1, keepdims=True)
    return (x * lax.rsqrt(ms + EPS) * nw) * (1.0 + sc) + sh


def _ada_kernel(c_ref, w_ref, b_ref, o_ref):
    s = _silu(c_ref[...])
    o_ref[...] = _dot(s.astype(BF16), w_ref[...].astype(BF16)) + b_ref[...]


def _ada_mods(cvec, ada_w, ada_b):
    depth, d, n = ada_w.shape
    tn = 1536
    return pl.pallas_call(
        _ada_kernel,
        grid=(depth, n // tn),
        in_specs=[
            pl.BlockSpec((SUBLANES, d), lambda l, j: (0, 0)),
            pl.BlockSpec((None, d, tn), lambda l, j: (l, 0, j)),
            pl.BlockSpec((None, 1, tn), lambda l, j: (l, 0, j)),
        ],
        out_specs=pl.BlockSpec((None, SUBLANES, tn), lambda l, j: (l, 0, j)),
        out_shape=jax.ShapeDtypeStruct((depth, SUBLANES, n), F32),
        compiler_params=_params("parallel", "parallel"),
        name="ada_mods",
    )(cvec, ada_w, ada_b.reshape(depth, 1, n))


def _in_proj_kernel(x_ref, nw_ref, sc_ref, sh_ref, w_ref, o_ref):
    hb = _norm_mod(x_ref[...], nw_ref[...], sc_ref[...], sh_ref[...]).astype(BF16)
    n = w_ref.shape[1]
    step = 512
    for c0 in range(0, n, step):
        cw = min(step, n - c0)
        o_ref[:, c0:c0 + cw] = _dot(hb, w_ref[:, c0:c0 + cw]).astype(o_ref.dtype)


def _in_proj(xs, nw, mods, w, tiles_per_batch, ctx_tiles):
    m, d = xs.shape
    n = w.shape[1]
    row = functools.partial(_mod_row, tiles_per_batch=tiles_per_batch, ctx_tiles=ctx_tiles)
    return pl.pallas_call(
        _in_proj_kernel,
        grid=(m // TM,),
        in_specs=[
            pl.BlockSpec((TM, d), lambda i: (i, 0)),
            pl.BlockSpec((1, d), lambda i: (0, 0)),
            pl.BlockSpec((None, 1, d), lambda i: (row(i), 0, 1)),
            pl.BlockSpec((None, 1, d), lambda i: (row(i), 0, 0)),
            pl.BlockSpec((d, n), lambda i: (0, 0)),
        ],
        out_specs=pl.BlockSpec((TM, n), lambda i: (i, 0)),
        out_shape=jax.ShapeDtypeStruct((m, n), F32),
        compiler_params=_params("parallel"),
        name="in_proj",
    )(xs, nw.reshape(1, d), mods, mods, w)


def _cumsum_rows(x, reverse):
    n = x.shape[0]
    row = lax.broadcasted_iota(jnp.int32, x.shape, 0)
    s = 1
    while s < n:
        if reverse:
            x = x + jnp.where(row < n - s, pltpu.roll(x, n - s, 0), 0.0)
        else:
            x = x + jnp.where(row >= s, pltpu.roll(x, s, 0), 0.0)
        s *= 2
    return x


def _bcast_row(x, r, rows):
    return jnp.broadcast_to(x[r:r + 1, :], (rows, x.shape[1]))


def _hg_chunk_head(qh, kh, vh, cum, st, reverse, masks):
    c = qh.shape[0]
    last = cum[0:1, :] if reverse else cum[c - 1:c, :]
    o = _dot_nt((qh * jnp.exp(cum)).astype(BF16), st.astype(BF16))
    kd = kh * jnp.exp(last - cum)
    st_new = st * jnp.exp(last) + _dot_tn(vh.astype(BF16), kd.astype(BF16))

    a = jnp.zeros((c, c), F32)
    half = c // 2
    li = 0
    while half >= HG_SUB:
        blk = 2 * half
        refs = []
        for b0 in range(0, c, blk):
            r = b0 + half if reverse else b0 + half - 1
            refs.append(_bcast_row(cum, r, blk))
        ref = refs[0] if len(refs) == 1 else jnp.concatenate(refs, axis=0)
        qs = qh * jnp.exp(cum - ref)
        ks = kh * jnp.exp(ref - cum)
        a = a + jnp.where(masks[li], _dot_nt(qs.astype(BF16), ks.astype(BF16)), 0.0)
        half //= 2
        li += 1

    pieces = []
    for g in range(c // HG_SUB):
        r0 = g * HG_SUB
        qg = qh[r0:r0 + HG_SUB, :]
        cg = cum[r0:r0 + HG_SUB, :]
        for j in range(HG_SUB):
            pieces.append(qg * jnp.exp(cg - _bcast_row(cum, r0 + j, HG_SUB)))
    rs = _dot_nt(jnp.concatenate(pieces, axis=0).astype(BF16), kh.astype(BF16))
    lane = lax.broadcasted_iota(jnp.int32, (HG_SUB, c), 1)
    rsub = lax.broadcasted_iota(jnp.int32, (HG_SUB, c), 0)
    diag = []
    for g in range(c // HG_SUB):
        acc = jnp.zeros((HG_SUB, c), F32)
        for j in range(HG_SUB):
            idx = g * HG_SUB + j
            causal = (rsub <= j) if reverse else (rsub >= j)
            acc = jnp.where((lane == idx) & causal, rs[idx * HG_SUB:(idx + 1) * HG_SUB, :], acc)
        diag.append(acc)
    a = a + jnp.concatenate(diag, axis=0)
    o = o + _dot(a.astype(BF16), vh.astype(BF16))
    return o, st_new


def _hg_kernel(*refs, reverse, final, rows):
    if final:
        q_ref, f_ref, v_ref, la_ref, lb_ref, g_ref, op_ref, nw_ref, o_ref, s_ref = refs
    else:
        q_ref, f_ref, v_ref, la_ref, lb_ref, o_ref, s_ref = refs

    @pl.when(pl.program_id(1) == 0)
    def _():
        s_ref[...] = jnp.zeros_like(s_ref)

    c = HG_CHUNK
    ti = lax.broadcasted_iota(jnp.int32, (c, c), 0)
    si = lax.broadcasted_iota(jnp.int32, (c, c), 1)
    masks = []
    half = c // 2
    while half >= HG_SUB:
        blk = 2 * half
        same = (ti & -blk) == (si & -blk)
        if reverse:
            m = same & ((ti & half) == 0) & ((si & half) != 0)
        else:
            m = same & ((ti & half) != 0) & ((si & half) == 0)
        masks.append(m)
        half //= 2

    n_ch = rows // c
    order = range(n_ch - 1, -1, -1) if reverse else range(n_ch)
    for ci in order:
        r0 = ci * c
        z = f_ref[r0:r0 + c, :]
        ls = jnp.minimum(z, 0.0) - jnp.log(1.0 + jnp.exp(-jnp.abs(z)))
        bt = lb_ref[...] + ls
        at = jnp.broadcast_to(la_ref[...], bt.shape)
        logf = jnp.maximum(at, bt) + jnp.log(1.0 + jnp.exp(-jnp.abs(at - bt)))
        kf = 1.0 - jnp.exp(logf)
        qf = _silu(q_ref[r0:r0 + c, :])
        vv = v_ref[r0:r0 + c, :]
        cum = _cumsum_rows(logf, reverse)
        outs = []
        for hh in range(HG_HEADS):
            sl = slice(hh * HG_DK, (hh + 1) * HG_DK)
            o, st_new = _hg_chunk_head(qf[:, sl], kf[:, sl], vv[:, sl], cum[:, sl], s_ref[hh],
                                       reverse, masks)
            s_ref[hh] = st_new
            if final:
                tot = o + op_ref[r0:r0 + c, sl]
                ms = jnp.mean(tot * tot, axis=-1, keepdims=True)
                o = tot * lax.rsqrt(ms + EPS) * nw_ref[:, sl]
            outs.append(o)
        o_all = jnp.concatenate(outs, axis=1)
        if final:
            o_all = o_all * _silu(g_ref[r0:r0 + c, :])
        o_ref[r0:r0 + c, :] = o_all.astype(o_ref.dtype)


def _hg_direction(u, la, lb, batch, n_ctx, reverse, o_prev=None, nw=None):
    m = u.shape[0]
    t = m // batch
    nblk = t // TM
    ncb = n_ctx // TM
    final = o_prev is not None
    cb = OFF_HG // HG_WIDTH

    def blk(j):
        if not reverse:
            return j
        return jnp.where(j < ncb, ncb - 1 - j, nblk - 1 - (j - ncb))

    def tok(col):
        return pl.BlockSpec((TM, HG_WIDTH), lambda b, j: (b * nblk + blk(j), col))

    vec = pl.BlockSpec((1, HG_WIDTH), lambda b, j: (0, 0))
    f_col = cb + 2 if reverse else cb + 1
    in_specs = [tok(cb), tok(f_col), tok(cb + 3), vec, vec]
    args = [u, u, u, la.reshape(1, -1), lb.reshape(1, -1)]
    if final:
        in_specs += [tok(cb + 4), tok(0), vec]
        args += [u, o_prev, nw.reshape(1, -1)]
    return pl.pallas_call(
        functools.partial(_hg_kernel, reverse=reverse, final=final, rows=TM),
        grid=(batch, nblk),
        in_specs=in_specs,
        out_specs=tok(0),
        out_shape=jax.ShapeDtypeStruct((m, HG_WIDTH), BF16 if final else F32),
        scratch_shapes=[pltpu.VMEM((HG_HEADS, HG_DK, HG_DK), F32)],
        compiler_params=_params("parallel", "arbitrary"),
        name="hgrn2_bwd" if reverse else "hgrn2_fwd",
    )(*args)


def _swap_halves(x):
    lane = lax.broadcasted_iota(jnp.int32, x.shape, 1)
    return jnp.where((lane & 32) == 0, pltpu.roll(x, LANES - 32, 1), pltpu.roll(x, 32, 1))


def _head_norm_rope(x, w, cos, sin, gmat):
    x2 = x * x
    hi = x2.astype(BF16)
    lo = (x2 - hi.astype(F32)).astype(BF16)
    ms = (_dot(hi, gmat) + _dot(lo, gmat)) * (1.0 / HEAD_DIM)
    xn = x * lax.rsqrt(ms + EPS) * w
    return xn * cos + _swap_halves(xn) * sin


def _att_prep_kernel(q_ref, k_ref, v_ref, cos_ref, sin_ref, qw_ref, kw_ref, g_ref,
                     qo_ref, k0_ref, k1_ref, v0_ref, v1_ref):
    cos, sin, gmat = cos_ref[...], sin_ref[...], g_ref[...]
    qscale = (HEAD_DIM ** -0.5) * float(np.log2(np.e))
    for j in range(ATT_WIDTH // LANES):
        sl = slice(j * LANES, (j + 1) * LANES)
        qr = _head_norm_rope(q_ref[:, sl], qw_ref[...], cos, sin, gmat)
        qo_ref[:, sl] = (qr * qscale).astype(BF16)
    kr = _head_norm_rope(k_ref[...], kw_ref[...], cos, sin, gmat)
    lane = lax.broadcasted_iota(jnp.int32, kr.shape, 1)
    k0_ref[...] = jnp.where(lane < HEAD_DIM, kr, 0.0).astype(BF16)
    k1_ref[...] = jnp.where(lane >= HEAD_DIM, kr, 0.0).astype(BF16)
    v = v_ref[...]
    v0_ref[...] = jnp.where(lane < HEAD_DIM, v, 1.0).astype(BF16)
    v1_ref[...] = jnp.where(lane >= HEAD_DIM, v, 1.0).astype(BF16)


def _att_prep(u, cos, sin, qw, kw, gmat, batch):
    m = u.shape[0]
    tpb = m // batch // TM
    kvw = ATT_KV_HEADS * HEAD_DIM
    tab = pl.BlockSpec((TM, LANES), lambda i: (i % tpb, 0))
    vec = pl.BlockSpec((1, LANES), lambda i: (0, 0))
    kv_out = pl.BlockSpec((TM, kvw), lambda i: (i, 0))
    return pl.pallas_call(
        _att_prep_kernel,
        grid=(m // TM,),
        in_specs=[
            pl.BlockSpec((TM, ATT_WIDTH), lambda i: (i, OFF_ATT_Q // ATT_WIDTH)),
            pl.BlockSpec((TM, kvw), lambda i: (i, OFF_ATT_K // kvw)),
            pl.BlockSpec((TM, kvw), lambda i: (i, OFF_ATT_V // kvw)),
            tab, tab, vec, vec,
            pl.BlockSpec((LANES, LANES), lambda i: (0, 0)),
        ],
        out_specs=[pl.BlockSpec((TM, ATT_WIDTH), lambda i: (i, 0)), kv_out, kv_out, kv_out, kv_out],
        out_shape=[jax.ShapeDtypeStruct((m, ATT_WIDTH), BF16)] + [jax.ShapeDtypeStruct((m, kvw), BF16)] * 4,
        compiler_params=_params("parallel"),
        name="att_prep",
    )(u, u, u, cos, sin, qw, kw, gmat)


def _att_kernel(bound_ref, q_ref, k0_ref, k1_ref, v0_ref, v1_ref, o_ref, *, n_ctx, n_all, tq):
    bound = bound_ref[0, 0]

    def attend(r0, rows, n_keys, use_bound):
        q = q_ref[pl.ds(r0, rows), :]

        def head(k_ref, v_ref):
            s = _dot_nt(q, k_ref[0:n_keys, :])
            shift = bound if use_bound else jnp.max(s, axis=-1, keepdims=True)
            return _dot(jnp.exp2(s - shift).astype(BF16), v_ref[0:n_keys, :])

        acc0, acc1 = head(k0_ref, v0_ref), head(k1_ref, v1_ref)
        lane = lax.broadcasted_iota(jnp.int32, acc0.shape, 1)
        out = jnp.where(lane < HEAD_DIM, acc0 / pltpu.roll(acc0, HEAD_DIM, 1),
                        acc1 / pltpu.roll(acc1, HEAD_DIM, 1))
        o_ref[pl.ds(r0, rows), :] = out.astype(o_ref.dtype)

    def all_rows(use_bound):
        if n_ctx:
            attend(0, n_ctx, n_ctx, use_bound)

        def lat_chunk(i, _):
            attend(pl.multiple_of(n_ctx + i * tq, TM), tq, n_all, use_bound)
            return 0

        lax.fori_loop(0, (n_all - n_ctx) // tq, lat_chunk, 0)

    safe = bound <= ATT_SAFE_BOUND

    @pl.when(safe)
    def _():
        all_rows(True)

    @pl.when(jnp.logical_not(safe))
    def _():
        all_rows(False)


def _attention(bound, qr, k0, k1, v0, v1, batch, n_ctx):
    m = qr.shape[0]
    t = m // batch
    kvw = ATT_KV_HEADS * HEAD_DIM
    tq = ATT_TQ if (t - n_ctx) % ATT_TQ == 0 else TM
    qspec = pl.BlockSpec((t, LANES), lambda b, j: (b, j))
    kvspec = pl.BlockSpec((t, kvw), lambda b, j: (b, 0))
    return pl.pallas_call(
        functools.partial(_att_kernel, n_ctx=n_ctx, n_all=t, tq=tq),
        grid=(batch, ATT_WIDTH // LANES),
        in_specs=[pl.BlockSpec(memory_space=pltpu.SMEM), qspec, kvspec, kvspec, kvspec, kvspec],
        out_specs=qspec,
        out_shape=jax.ShapeDtypeStruct((m, ATT_WIDTH), BF16),
        compiler_params=_params("parallel", "parallel"),
        name="attention",
    )(bound, qr, k0, k1, v0, v1)


def _scan_steps(a, u, reverse, axis):
    n = a.shape[axis]
    pos = lax.broadcasted_iota(jnp.int32, a.shape, axis)
    s = 1
    while s < n:
        if reverse:
            keep = pos < n - s
            ash, ush = pltpu.roll(a, n - s, axis), pltpu.roll(u, n - s, axis)
        else:
            keep = pos >= s
            ash, ush = pltpu.roll(a, s, axis), pltpu.roll(u, s, axis)
        u = u + a * jnp.where(keep, ush, 0.0)
        a = a * jnp.where(keep, ash, 1.0)
        s *= 2
    return a, u


def _scan_rows(a, u, carry, reverse, sa_ref, su_ref):
    n, width = a.shape
    groups = n // SUBLANES
    a3, u3 = _scan_steps(a.reshape(groups, SUBLANES, width), u.reshape(groups, SUBLANES, width), reverse, 1)
    a, u = a3.reshape(n, width), u3.reshape(n, width)
    sa_ref[...] = a
    su_ref[...] = u
    edge = 0 if reverse else SUBLANES - 1
    ae = sa_ref[pl.ds(edge, groups, stride=SUBLANES), :]
    ue = su_ref[pl.ds(edge, groups, stride=SUBLANES), :]
    ae, ue = _scan_steps(ae, ue, reverse, 0)
    hc = ue + ae * carry
    grow = lax.broadcasted_iota(jnp.int32, hc.shape, 0)
    if reverse:
        cin = jnp.where(grow < groups - 1, pltpu.roll(hc, groups - 1, 0), carry)
        out = hc[0:1, :]
    else:
        cin = jnp.where(grow >= 1, pltpu.roll(hc, 1, 0), carry)
        out = hc[groups - 1:groups, :]
    cin_rows = jnp.concatenate([_bcast_row(cin, i, SUBLANES) for i in range(groups)], axis=0)
    return u + a * cin_rows, out


def _lru_kernel(x_ref, g_ref, cw_ref, cb_ref, w_ref, b_ref, lam_ref, o_ref, xp_ref, xc_ref, h_ref,
                sa_ref, su_ref, *, n_ctx, n_tok):
    tt = TM
    pad = SUBLANES
    n_lat = n_tok - n_ctx
    lat0 = n_ctx + 2 * pad
    zeros = jnp.zeros((pad, LANES), F32)
    xp_ref[0:pad, :] = zeros
    xp_ref[pad + n_ctx:lat0, :] = zeros
    xp_ref[lat0 + n_lat:lat0 + n_lat + pad, :] = zeros
    xp_ref[pad:pad + n_ctx, :] = x_ref[0:n_ctx, :]
    xp_ref[lat0:lat0 + n_lat, :] = x_ref[n_ctx:n_tok, :]

    n_tiles = n_tok // tt
    ctx_tiles = n_ctx // tt

    def conv_tile(i, _):
        r0 = pl.multiple_of(i * tt, tt)
        p0 = pl.multiple_of(jnp.where(i < ctx_tiles, r0, r0 + pad), pad)
        ext = xp_ref[pl.ds(p0, tt + 2 * pad), :]
        acc = cb_ref[...] + cw_ref[0:1, :] * ext[pad - 2:pad - 2 + tt, :]
        for j in range(1, CONV_W):
            acc = acc + cw_ref[j:j + 1, :] * ext[pad - 2 + j:pad - 2 + j + tt, :]
        xc_ref[pl.ds(r0, tt), :] = acc
        return 0

    lax.fori_loop(0, n_tiles, conv_tile, 0)

    lam = lam_ref[...]
    sp = jnp.maximum(-lam, 0.0) + jnp.log1p(jnp.exp(-jnp.abs(lam)))

    def coeffs(r0, d):
        xc = xc_ref[pl.ds(r0, tt), :]
        pre = _dot(xc.astype(BF16), w_ref[:, d * 2 * LANES:(d + 1) * 2 * LANES]) \
            + b_ref[:, d * 2 * LANES:(d + 1) * 2 * LANES]
        r = _sigmoid(pre[:, :LANES])
        ig = _sigmoid(pre[:, LANES:])
        log_a = (-RG_C) * r * sp[d:d + 1, :]
        a = jnp.exp(log_a)
        t = 1.0 - a * a
        root = jnp.where(t > 0.0, t * lax.rsqrt(t), 0.0)
        return a, root * (ig * xc)

    def fwd_tile(i, carry):
        r0 = pl.multiple_of(i * tt, tt)
        a, u = coeffs(r0, 0)
        h, out = _scan_rows(a, u, carry, False, sa_ref, su_ref)
        h_ref[pl.ds(r0, tt), :] = h
        return out

    lax.fori_loop(0, n_tiles, fwd_tile, jnp.zeros((1, LANES), F32))

    def bwd_tile(j, carry):
        i = jnp.where(j < ctx_tiles, ctx_tiles - 1 - j, n_tiles - 1 - (j - ctx_tiles))
        r0 = pl.multiple_of(i * tt, tt)
        a, u = coeffs(r0, 1)
        h, out = _scan_rows(a, u, carry, True, sa_ref, su_ref)
        tot = h + h_ref[pl.ds(r0, tt), :]
        o_ref[pl.ds(r0, tt), :] = (tot * jax.nn.gelu(g_ref[pl.ds(r0, tt), :])).astype(o_ref.dtype)
        return out

    lax.fori_loop(0, n_tiles, bwd_tile, jnp.zeros((1, LANES), F32))


def _rglru(u, conv_w, conv_b, wcat, bcat, lam, batch, n_ctx):
    m = u.shape[0]
    t = m // batch
    ncg = LRU_WIDTH // LANES
    xcol = OFF_LRU // LANES
    return pl.pallas_call(
        functools.partial(_lru_kernel, n_ctx=n_ctx, n_tok=t),
        grid=(batch, ncg),
        in_specs=[
            pl.BlockSpec((t, LANES), lambda b, c: (b, xcol + c)),
            pl.BlockSpec((t, LANES), lambda b, c: (b, xcol + ncg + c)),
            pl.BlockSpec((CONV_W, LANES), lambda b, c: (0, c)),
            pl.BlockSpec((1, LANES), lambda b, c: (0, c)),
            pl.BlockSpec((None, LANES, 4 * LANES), lambda b, c: (c, 0, 0)),
            pl.BlockSpec((None, 1, 4 * LANES), lambda b, c: (c, 0, 0)),
            pl.BlockSpec((2, LANES), lambda b, c: (0, c)),
        ],
        out_specs=pl.BlockSpec((t, LANES), lambda b, c: (b, c)),
        out_shape=jax.ShapeDtypeStruct((m, LRU_WIDTH), BF16),
        scratch_shapes=[pltpu.VMEM((t + 3 * SUBLANES, LANES), F32),
                        pltpu.VMEM((t, LANES), F32),
                        pltpu.VMEM((t, LANES), F32),
                        pltpu.VMEM((TM, LANES), F32),
                        pltpu.VMEM((TM, LANES), F32)],
        compiler_params=_params("parallel", "parallel"),
        name="rglru",
    )(u, u, conv_w, conv_b.reshape(1, -1), wcat, bcat, lam)


def _merge_kernel(ya_ref, yb_ref, yc_ref, ml_ref, xs_ref, g1_ref, wa_ref, wb_ref, wc_ref, wo_ref, o_ref):
    d = D_MODEL
    acc = _sigmoid(ml_ref[:, 0:d]) * _dot(ya_ref[...], wa_ref[...])
    acc = acc + _sigmoid(ml_ref[:, d:2 * d]) * _dot(yb_ref[...], wb_ref[...])
    acc = acc + _sigmoid(ml_ref[:, 2 * d:3 * d]) * _dot(yc_ref[...], wc_ref[...])
    y = _dot(acc.astype(BF16), wo_ref[...])
    o_ref[...] = xs_ref[...] + g1_ref[...] * y


def _merge(ya, yb, yc, u, xs, mods, wa, wb, wc, wo, batch, n_ctx, drop_ctx):
    m, d = xs.shape
    tpb = m // batch // TM
    ctx_tiles = n_ctx // TM
    if drop_ctx:
        out_tpb = tpb - ctx_tiles
        src = lambda i: (i // out_tpb) * tpb + ctx_tiles + i % out_tpb
        row = lambda i: i // out_tpb
        m_out = batch * out_tpb * TM
    else:
        src = lambda i: i
        row = functools.partial(_mod_row, tiles_per_batch=tpb, ctx_tiles=ctx_tiles)
        m_out = m
    br = pl.BlockSpec((TM, HG_WIDTH), lambda i: (src(i), 0))
    full = lambda a: pl.BlockSpec(a.shape, lambda i: (0, 0))
    return pl.pallas_call(
        _merge_kernel,
        grid=(m_out // TM,),
        in_specs=[br, br, br,
                  pl.BlockSpec((TM, 3 * d), lambda i: (src(i), OFF_MERGE)),
                  pl.BlockSpec((TM, d), lambda i: (src(i), 0)),
                  pl.BlockSpec((None, 1, d), lambda i: (row(i), 0, 2)),
                  full(wa), full(wb), full(wc), full(wo)],
        out_specs=pl.BlockSpec((TM, d), lambda i: (i, 0)),
        out_shape=jax.ShapeDtypeStruct((m_out, d), F32),
        compiler_params=_params("parallel"),
        name="merge_out",
    )(ya, yb, yc, u, xs, mods, wa, wb, wc, wo)


def _swiglu_acc(hb, wg_ref, wu_ref, wd_ref):
    acc = jnp.zeros((hb.shape[0], wd_ref.shape[1]), F32)
    for c0 in range(0, D_FF, FF_CHUNK):
        a = _dot(hb, wg_ref[:, c0:c0 + FF_CHUNK])
        b = _dot(hb, wu_ref[:, c0:c0 + FF_CHUNK])
        acc = acc + _dot((_silu(a) * b).astype(BF16), wd_ref[c0:c0 + FF_CHUNK, :])
    return acc


def _final_norm(y, fw):
    ms = jnp.mean(y * y, axis=-1, keepdims=True)
    return y * lax.rsqrt(ms + EPS) * fw


def _ffn_kernel(x_ref, nw_ref, sc_ref, sh_ref, g2_ref, wg_ref, wu_ref, wd_ref, fw_ref, o_ref, *, final):
    x = x_ref[...]
    hb = _norm_mod(x, nw_ref[...], sc_ref[...], sh_ref[...]).astype(BF16)
    y = x + g2_ref[...] * _swiglu_acc(hb, wg_ref, wu_ref, wd_ref)
    o_ref[...] = _final_norm(y, fw_ref[...]) if final else y


def _mod_specs(row, d):
    return [pl.BlockSpec((None, 1, d), lambda i, *_: (row(i), 0, 4)),
            pl.BlockSpec((None, 1, d), lambda i, *_: (row(i), 0, 3)),
            pl.BlockSpec((None, 1, d), lambda i, *_: (row(i), 0, 5))]


def _ffn(xs, nw, mods, wg, wu, wd, fw, tiles_per_batch, ctx_tiles, final):
    m, d = xs.shape
    row = functools.partial(_mod_row, tiles_per_batch=tiles_per_batch, ctx_tiles=ctx_tiles)
    full = lambda a: pl.BlockSpec(a.shape, lambda i: (0, 0))
    vec = pl.BlockSpec((1, d), lambda i: (0, 0))
    return pl.pallas_call(
        functools.partial(_ffn_kernel, final=final),
        grid=(m // TM,),
        in_specs=[pl.BlockSpec((TM, d), lambda i: (i, 0)), vec] + _mod_specs(row, d)
                 + [full(wg), full(wu), full(wd), vec],
        out_specs=pl.BlockSpec((TM, d), lambda i: (i, 0)),
        out_shape=jax.ShapeDtypeStruct((m, d), F32),
        compiler_params=_params("parallel"),
        name="ffn_dense",
    )(xs, nw.reshape(1, d), mods, mods, mods, wg, wu, wd, fw.reshape(1, d))


def _top2(logits):
    lane = lax.broadcasted_iota(jnp.int32, logits.shape, 1)
    m1 = jnp.max(logits, axis=-1, keepdims=True)
    i1 = jnp.min(jnp.where(logits == m1, lane, LANES), axis=-1, keepdims=True)
    rest = jnp.where(lane == i1, -jnp.inf, logits)
    m2 = jnp.max(rest, axis=-1, keepdims=True)
    i2 = jnp.min(jnp.where(rest == m2, lane, LANES), axis=-1, keepdims=True)
    e2 = jnp.exp(m2 - m1)
    w1 = 1.0 / (1.0 + e2)
    return i1, i2, w1, e2 * w1


def _router_kernel(x_ref, nw_ref, sc_ref, sh_ref, rw_ref, meta_ref, wts_ref, cnt_ref, base_ref):
    @pl.when(pl.program_id(0) == 0)
    def _():
        base_ref[...] = jnp.zeros_like(base_ref)

    h = _norm_mod(x_ref[...], nw_ref[...], sc_ref[...], sh_ref[...])
    logits = jnp.dot(h, rw_ref[...], preferred_element_type=F32, precision=lax.Precision.HIGHEST)
    lane = lax.broadcasted_iota(jnp.int32, logits.shape, 1)
    i1, i2, w1, w2 = _top2(jnp.where(lane < N_EXPERTS, logits, -jnp.inf))
    hit = jnp.where((lane == i1) | (lane == i2), 1.0, 0.0)
    tm = hit.shape[0]
    tri = lax.broadcasted_iota(jnp.int32, (tm, tm), 0) > lax.broadcasted_iota(jnp.int32, (tm, tm), 1)
    before = _dot(jnp.where(tri, 1.0, 0.0).astype(BF16), hit.astype(BF16)) + base_ref[...]
    r1 = jnp.sum(jnp.where(lane == i1, before, 0.0), axis=-1, keepdims=True).astype(jnp.int32)
    r2 = jnp.sum(jnp.where(lane == i2, before, 0.0), axis=-1, keepdims=True).astype(jnp.int32)
    base_ref[...] += jnp.sum(hit, axis=0, keepdims=True)
    cnt_ref[...] = base_ref[...]
    meta_ref[...] = jnp.where(lane == 0, i1, jnp.where(lane == 1, i2, jnp.where(lane == 2, r1,
                              jnp.where(lane == 3, r2, 0))))
    wts_ref[...] = jnp.where(lane == 0, w1, jnp.where(lane == 1, w2, 0.0))


def _router(xs, nw, mods, rw, row):
    m, d = xs.shape
    vec = pl.BlockSpec((1, d), lambda i: (0, 0))
    tok = pl.BlockSpec((TM, LANES), lambda i: (i, 0))
    return pl.pallas_call(
        _router_kernel,
        grid=(m // TM,),
        in_specs=[pl.BlockSpec((TM, d), lambda i: (i, 0)), vec] + _mod_specs(row, d)[:2]
                 + [pl.BlockSpec((d, LANES), lambda i: (0, 0))],
        out_specs=[tok, tok, pl.BlockSpec((1, LANES), lambda i: (0, 0))],
        out_shape=[jax.ShapeDtypeStruct((m, LANES), jnp.int32), jax.ShapeDtypeStruct((m, LANES), F32),
                   jax.ShapeDtypeStruct((1, LANES), F32)],
        scratch_shapes=[pltpu.VMEM((1, LANES), F32)],
        compiler_params=_params("arbitrary"),
        name="moe_router",
    )(xs, nw.reshape(1, d), mods, mods, rw)


def _row_copy(src_ref, s, dst_ref, t, sem):
    return pltpu.make_async_copy(src_ref.at[pl.ds(s, 1)], dst_ref.at[pl.ds(t, 1)], sem)


def _dispatch_kernel(pos_ref, x_ref, nw_ref, sc_ref, sh_ref, xin_ref, xbuf_ref, h_ref, sem):
    del xin_ref
    h_ref[...] = _norm_mod(x_ref[...], nw_ref[...], sc_ref[...], sh_ref[...])
    tm = h_ref.shape[0]

    def issue(r, _):
        _row_copy(h_ref, r, xbuf_ref, pos_ref[0, 2 * r], sem).start(priority=0)
        _row_copy(h_ref, r, xbuf_ref, pos_ref[0, 2 * r + 1], sem).start(priority=1)
        return 0

    lax.fori_loop(0, tm, issue, 0, unroll=DMA_UNROLL)

    def drain(r, _):
        _row_copy(h_ref, 0, xbuf_ref, 0, sem).wait()
        _row_copy(h_ref, 0, xbuf_ref, 0, sem).wait()
        return 0

    lax.fori_loop(0, tm, drain, 0, unroll=DMA_UNROLL)


def _dispatch(xs, nw, mods, pos, row, n_rows):
    m, d = xs.shape
    vec = pl.BlockSpec((1, d), lambda i: (0, 0))
    slab = (d,)
    zeros = jnp.zeros((n_rows,) + slab, F32)
    return pl.pallas_call(
        _dispatch_kernel,
        grid=(m // TM,),
        in_specs=[pl.BlockSpec((None, 1, 2 * TM), lambda i: (i, 0, 0), memory_space=pltpu.SMEM),
                  pl.BlockSpec((TM, d), lambda i: (i, 0)), vec] + _mod_specs(row, d)[:2]
                 + [pl.BlockSpec(memory_space=pl.ANY)],
        out_specs=pl.BlockSpec(memory_space=pl.ANY),
        out_shape=jax.ShapeDtypeStruct((n_rows,) + slab, F32),
        scratch_shapes=[pltpu.VMEM((TM,) + slab, F32), pltpu.SemaphoreType.DMA(())],
        input_output_aliases={5: 0},
        compiler_params=_params("arbitrary"),
        name="moe_dispatch",
    )(pos, xs, nw.reshape(1, d), mods, mods, zeros)


def _expert_ffn_kernel(te_ref, nu_ref, x_ref, wg_ref, wu_ref, wd_ref, y_ref):
    del te_ref
    used = pl.program_id(0) < nu_ref[0]

    @pl.when(used)
    def _():
        y_ref[...] = _swiglu_acc(x_ref[...].astype(BF16), wg_ref, wu_ref, wd_ref)

    @pl.when(jnp.logical_not(used))
    def _():
        y_ref[...] = jnp.zeros_like(y_ref)


def _expert_ffn(xbuf, tile_expert, n_used, wg, wu, wd):
    n_rows = xbuf.shape[0]
    d = wg.shape[1]
    wspec = lambda shape: pl.BlockSpec((None,) + shape, lambda i, te, nu: (te[i], 0, 0))
    tok = pl.BlockSpec((TM, d), lambda i, te, nu: (i, 0))
    return pl.pallas_call(
        _expert_ffn_kernel,
        grid_spec=pltpu.PrefetchScalarGridSpec(
            num_scalar_prefetch=2,
            grid=(n_rows // TM,),
            in_specs=[tok, wspec((d, D_FF)), wspec((d, D_FF)), wspec((D_FF, d))],
            out_specs=tok,
        ),
        out_shape=jax.ShapeDtypeStruct(xbuf.shape, F32),
        compiler_params=_params("arbitrary"),
        name="moe_expert_ffn",
    )(tile_expert, n_used, xbuf, wg, wu, wd)


def _combine_kernel(pos_ref, x_ref, wts_ref, g2_ref, fw_ref, y_ref, o_ref, buf_ref, sem, *, final):
    tm = x_ref.shape[0]

    def issue(r, _):
        _row_copy(y_ref, pos_ref[0, 2 * r], buf_ref.at[0], r, sem).start(priority=0)
        _row_copy(y_ref, pos_ref[0, 2 * r + 1], buf_ref.at[1], r, sem).start(priority=1)
        return 0

    lax.fori_loop(0, tm, issue, 0, unroll=DMA_UNROLL)

    def drain(r, _):
        _row_copy(y_ref, 0, buf_ref.at[0], 0, sem).wait()
        _row_copy(y_ref, 0, buf_ref.at[0], 0, sem).wait()
        return 0

    lax.fori_loop(0, tm, drain, 0, unroll=DMA_UNROLL)
    w = wts_ref[...]
    f = w[:, 0:1] * buf_ref[0] + w[:, 1:2] * buf_ref[1]
    y = x_ref[...] + g2_ref[...] * f
    o_ref[...] = _final_norm(y, fw_ref[...]) if final else y


def _combine(xs, wts, mods, fw, ybuf, pos, row, final):
    m, d = xs.shape
    vec = pl.BlockSpec((1, d), lambda i: (0, 0))
    return pl.pallas_call(
        functools.partial(_combine_kernel, final=final),
        grid=(m // TM,),
        in_specs=[pl.BlockSpec((None, 1, 2 * TM), lambda i: (i, 0, 0), memory_space=pltpu.SMEM),
                  pl.BlockSpec((TM, d), lambda i: (i, 0)),
                  pl.BlockSpec((TM, LANES), lambda i: (i, 0)),
                  _mod_specs(row, d)[2], vec,
                  pl.BlockSpec(memory_space=pl.ANY)],
        out_specs=pl.BlockSpec((TM, d), lambda i: (i, 0)),
        out_shape=jax.ShapeDtypeStruct((m, d), F32),
        scratch_shapes=[pltpu.VMEM((2, TM) + ybuf.shape[1:], F32), pltpu.SemaphoreType.DMA(())],
        compiler_params=_params("arbitrary"),
        name="moe_combine",
    )(pos, xs, wts, mods, fw.reshape(1, d), ybuf)


def _moe(xs, nw, mods, rw, wg, wu, wd, fw, tiles_per_batch, ctx_tiles, final):
    m, d = xs.shape
    row = functools.partial(_mod_row, tiles_per_batch=tiles_per_batch, ctx_tiles=ctx_tiles)
    meta, wts, cnt = _router(xs, nw, mods, rw, row)
    counts = cnt[0, :N_EXPERTS].astype(jnp.int32)
    padded = (counts + TM - 1) // TM * TM
    ends = jnp.cumsum(padded)
    starts = ends - padded
    pos = jnp.stack([starts[meta[:, 0]] + meta[:, 2], starts[meta[:, 1]] + meta[:, 3]], axis=-1)
    pos = pos.reshape(m // TM, 1, 2 * TM)
    n_rows = 2 * m + N_EXPERTS * TM
    n_used = (ends[-1] // TM).astype(jnp.int32)
    tile_start = jnp.arange(n_rows // TM, dtype=jnp.int32) * TM
    tile_expert = jnp.searchsorted(ends, jnp.minimum(tile_start, ends[-1] - 1), side="right").astype(jnp.int32)
    xbuf = _dispatch(xs, nw, mods, pos, row, n_rows)
    ybuf = _expert_ffn(xbuf, tile_expert, n_used.reshape(1), wg, wu, wd)
    return _combine(xs, wts, mods, fw, ybuf, pos, row, final)


def _column_perm():
    sizes = (512, 512, 512, 512, 512, 512, 128, 128, 512, 512, 3 * D_MODEL)
    starts = np.concatenate([[0], np.cumsum(sizes)[:-1]])
    a_q, a_ff, a_fb, a_v, a_g, b_q, b_k, b_v, c_x, c_g, mrg = [
        np.arange(s, s + n) for s, n in zip(starts, sizes)]
    half = np.concatenate([np.arange(0, HEAD_DIM, 2), np.arange(1, HEAD_DIM, 2)])
    group = ATT_HEADS // ATT_KV_HEADS
    q_cols = []
    for j in range(group):
        q_cols += [b_q[j * HEAD_DIM + half], b_q[(group + j) * HEAD_DIM + half]]
    k_cols = [b_k[h * HEAD_DIM + half] for h in range(ATT_KV_HEADS)]
    return np.concatenate([mrg, a_q, a_ff, a_fb, a_v, a_g, c_x, c_g] + q_cols + k_cols + [b_v])


def _permute_columns(w, perm):
    cuts = np.flatnonzero(np.diff(perm) != 1) + 1
    parts, pending = [], []
    for run in np.split(perm, cuts):
        if len(run) >= LANES:
            if pending:
                parts.append(w[:, np.concatenate(pending)])
                pending = []
            parts.append(w[:, int(run[0]):int(run[-1]) + 1])
        else:
            pending.append(run)
    if pending:
        parts.append(w[:, np.concatenate(pending)])
    return jnp.concatenate(parts, axis=1)


def _att_out_rows():
    group = ATT_HEADS // ATT_KV_HEADS
    idx = []
    for j in range(group):
        idx += [np.arange(j * HEAD_DIM, (j + 1) * HEAD_DIM),
                np.arange((group + j) * HEAD_DIM, (group + j + 1) * HEAD_DIM)]
    return np.concatenate(idx)


def _rope_tables(n_ctx, n_lat):
    rows = n_lat // GRID_W
    row = jnp.repeat(jnp.arange(rows, dtype=F32), GRID_W)
    col = jnp.tile(jnp.arange(GRID_W, dtype=F32), rows)
    pairs = HEAD_DIM // 4
    freqs = ROPE_THETA ** (-jnp.arange(pairs, dtype=F32) / pairs)
    ang = jnp.concatenate([row[:, None] * freqs, col[:, None] * freqs], axis=-1)
    cos, sin = jnp.cos(ang), jnp.sin(ang)
    cos = jnp.concatenate([jnp.ones((n_ctx, HEAD_DIM // 2), F32), cos], axis=0)
    sin = jnp.concatenate([jnp.zeros((n_ctx, HEAD_DIM // 2), F32), sin], axis=0)
    reps = LANES // HEAD_DIM
    return (jnp.tile(jnp.concatenate([cos, cos], axis=-1), (1, reps)),
            jnp.tile(jnp.concatenate([-sin, sin], axis=-1), (1, reps)))


def _lru_weights(wa, wx, ba, bx):
    ncg = LRU_WIDTH // LANES

    def dense(w):
        w = w.reshape(ncg, 2, LRU_BLOCK_W, LRU_BLOCK_W)
        out = jnp.zeros((ncg, LANES, LANES), w.dtype)
        out = out.at[:, :LRU_BLOCK_W, :LRU_BLOCK_W].set(w[:, 0])
        return out.at[:, LRU_BLOCK_W:, LRU_BLOCK_W:].set(w[:, 1])

    wcat = jnp.concatenate([dense(wa[0]), dense(wx[0]), dense(wa[1]), dense(wx[1])], axis=-1).astype(BF16)
    bcat = jnp.concatenate([b.reshape(ncg, 1, LANES) for b in (ba[0], bx[0], ba[1], bx[1])], axis=-1)
    return wcat, bcat


def kernel(x, c, ctx, c_ctx, ada_w, ada_b, mix_norm_w, ffn_norm_w, w_in, hg_lb_logits, hg_norm_w, q_norm_w, k_norm_w, lru_conv_w, lru_conv_b, lru_wa, lru_ba, lru_wx, lru_bx, lru_lambda, w_br_a, w_br_b, w_br_c, w_out, ffn_w_gate, ffn_w_up, ffn_w_down, router_w, moe_w_gate, moe_w_up, moe_w_down, final_norm_w):
    batch, n_lat, d = x.shape
    n_ctx = ctx.shape[1]
    t = n_ctx + n_lat
    depth = ada_w.shape[0]
    assert n_ctx % TM == 0 and n_lat % TM == 0 and batch <= 4

    cvec = jnp.zeros((SUBLANES, d), F32).at[:batch].set(c).at[4].set(c_ctx)
    mods_all = _ada_mods(cvec, ada_w, ada_b).reshape(depth, SUBLANES, 1, 6 * d)

    p = jax.nn.softmax(hg_lb_logits.astype(F32), axis=0)
    cum = jnp.cumsum(p, axis=0)
    lbs = cum - cum[:1]
    log_lb, log_1m_lb = jnp.log(lbs), jnp.log1p(-lbs)

    cos, sin = _rope_tables(n_ctx, n_lat)
    half = np.concatenate([np.arange(0, HEAD_DIM, 2), np.arange(1, HEAD_DIM, 2)])
    gidx = np.arange(LANES) // HEAD_DIM
    gmat = jnp.asarray(gidx[:, None] == gidx[None, :], BF16)
    col_perm = _column_perm()
    att_rows = _att_out_rows()

    xs = jnp.concatenate([ctx.astype(x.dtype), x], axis=1).reshape(batch * t, d)
    tpb, ctx_tiles = t // TM, n_ctx // TM
    for i in range(depth):
        last = i == depth - 1
        mods = mods_all[i]
        w_i = _permute_columns(w_in[i], col_perm).astype(BF16)
        u = _in_proj(xs, mix_norm_w[i], mods, w_i, tpb, ctx_tiles)

        o_f = _hg_direction(u, log_lb[i, 0], log_1m_lb[i, 0], batch, n_ctx, False)
        y_a = _hg_direction(u, log_lb[i, 1], log_1m_lb[i, 1], batch, n_ctx, True, o_f, hg_norm_w[i])

        reps = LANES // HEAD_DIM
        qw = jnp.tile(q_norm_w[i][half], reps).reshape(1, LANES)
        kw = jnp.tile(k_norm_w[i][half], reps).reshape(1, LANES)
        qr, k0, k1, v0, v1 = _att_prep(u, cos, sin, qw, kw, gmat, batch)
        bound = (1.02 * HEAD_DIM ** 0.5 * float(np.log2(np.e))
                 * jnp.max(jnp.abs(q_norm_w[i])) * jnp.max(jnp.abs(k_norm_w[i]))).reshape(1, 1)
        y_b = _attention(bound, qr, k0, k1, v0, v1, batch, n_ctx)

        wcat, bcat = _lru_weights(lru_wa[i], lru_wx[i], lru_ba[i], lru_bx[i])
        y_c = _rglru(u, lru_conv_w[i], lru_conv_b[i], wcat, bcat, lru_lambda[i], batch, n_ctx)

        xs = _merge(y_a, y_b, y_c, u, xs, mods, w_br_a[i].astype(BF16), w_br_b[i][att_rows].astype(BF16),
                    w_br_c[i].astype(BF16), w_out[i].astype(BF16), batch, n_ctx, last)
        if last:
            tpb, ctx_tiles = n_lat // TM, 0
        if i % 2 == 0:
            j = i // 2
            xs = _ffn(xs, ffn_norm_w[i], mods, ffn_w_gate[j].astype(BF16), ffn_w_up[j].astype(BF16),
                      ffn_w_down[j].astype(BF16), final_norm_w, tpb, ctx_tiles, last)
        else:
            j = i // 2
            rw = jnp.zeros((d, LANES), F32).at[:, :N_EXPERTS].set(router_w[j])
            xs = _moe(xs, ffn_norm_w[i], mods, rw, moe_w_gate[j].astype(BF16), moe_w_up[j].astype(BF16),
                      moe_w_down[j].astype(BF16), final_norm_w, tpb, ctx_tiles, last)
    return xs.reshape(batch, n_lat, d)
```

```python
import functools

import numpy as np
import jax
import jax.numpy as jnp
from jax import lax
from jax.experimental import pallas as pl
from jax.experimental.pallas import tpu as pltpu

F32 = jnp.float32
BF16 = jnp.bfloat16

D_MODEL = 1024
DEPTH = 4
GRID_W = 64
HG_HEADS = 4
HG_DK = 128
HG_WIDTH = 512
ATT_HEADS = 8
ATT_KV_HEADS = 2
HEAD_DIM = 64
ATT_WIDTH = 512
ROPE_THETA = 10000.0
LRU_WIDTH = 512
LRU_BLOCK_W = 64
CONV_W = 4
RG_C = 8.0
D_FF = 2816
N_EXPERTS = 8
EPS = 1e-6

LANES = 128
SUBLANES = 8
TM = 256
HG_CHUNK = 64
HG_SUB = 8
ATT_TQ = 512
ATT_SAFE_BOUND = 60.0
FF_CHUNK = 1408
DMA_UNROLL = 8
VMEM_LIMIT = 56 * 1024 * 1024

OFF_MERGE = 0
OFF_HG = 3 * D_MODEL
OFF_LRU = OFF_HG + 5 * HG_WIDTH
OFF_ATT_Q = OFF_LRU + 2 * LRU_WIDTH
OFF_ATT_K = OFF_ATT_Q + ATT_WIDTH
OFF_ATT_V = OFF_ATT_K + ATT_KV_HEADS * HEAD_DIM
IN_DIM = OFF_ATT_V + ATT_KV_HEADS * HEAD_DIM


def _dot(a, b):
    return jnp.dot(a, b, preferred_element_type=F32)


def _dot_nt(a, b):
    return lax.dot_general(a, b, (((1,), (1,)), ((), ())), preferred_element_type=F32)


def _dot_tn(a, b):
    return lax.dot_general(a, b, (((0,), (0,)), ((), ())), preferred_element_type=F32)


def _sigmoid(x):
    return 1.0 / (1.0 + jnp.exp(-x))


def _silu(x):
    return x * _sigmoid(x)


def _params(*sem):
    return pltpu.CompilerParams(dimension_semantics=sem, vmem_limit_bytes=VMEM_LIMIT)


def _mod_row(i, tiles_per_batch, ctx_tiles):
    if ctx_tiles == 0:
        return i // tiles_per_batch
    return jnp.where(i % tiles_per_batch < ctx_tiles, 4, i // tiles_per_batch)


def _norm_mod(x, nw, sc, sh):
    ms = jnp.mean(x * x, axis=-1, keepdims=True)
    return (x * lax.rsqrt(ms + EPS) * nw) * (1.0 + sc) + sh


def _ada_kernel(c_ref, w_ref, b_ref, o_ref):
    s = _silu(c_ref[...])
    o_ref[...] = _dot(s.astype(BF16), w_ref[...].astype(BF16)) + b_ref[...]


def _ada_mods(cvec, ada_w, ada_b):
    depth, d, n = ada_w.shape
    tn = 1536
    return pl.pallas_call(
        _ada_kernel,
        grid=(depth, n // tn),
        in_specs=[
            pl.BlockSpec((SUBLANES, d), lambda l, j: (0, 0)),
            pl.BlockSpec((None, d, tn), lambda l, j: (l, 0, j)),
            pl.BlockSpec((None, 1, tn), lambda l, j: (l, 0, j)),
        ],
        out_specs=pl.BlockSpec((None, SUBLANES, tn), lambda l, j: (l, 0, j)),
        out_shape=jax.ShapeDtypeStruct((depth, SUBLANES, n), F32),
        compiler_params=_params("parallel", "parallel"),
        name="ada_mods",
    )(cvec, ada_w, ada_b.reshape(depth, 1, n))


def _in_proj_kernel(x_ref, nw_ref, sc_ref, sh_ref, w_ref, o_ref):
    hb = _norm_mod(x_ref[...], nw_ref[...], sc_ref[...], sh_ref[...]).astype(BF16)
    n = w_ref.shape[1]
    step = 512
    for c0 in range(0, n, step):
        cw = min(step, n - c0)
        o_ref[:, c0:c0 + cw] = _dot(hb, w_ref[:, c0:c0 + cw]).astype(o_ref.dtype)


def _in_proj(xs, nw, mods, w, tiles_per_batch, ctx_tiles):
    m, d = xs.shape
    n = w.shape[1]
    row = functools.partial(_mod_row, tiles_per_batch=tiles_per_batch, ctx_tiles=ctx_tiles)
    return pl.pallas_call(
        _in_proj_kernel,
        grid=(m // TM,),
        in_specs=[
            pl.BlockSpec((TM, d), lambda i: (i, 0)),
            pl.BlockSpec((1, d), lambda i: (0, 0)),
            pl.BlockSpec((None, 1, d), lambda i: (row(i), 0, 1)),
            pl.BlockSpec((None, 1, d), lambda i: (row(i), 0, 0)),
            pl.BlockSpec((d, n), lambda i: (0, 0)),
        ],
        out_specs=pl.BlockSpec((TM, n), lambda i: (i, 0)),
        out_shape=jax.ShapeDtypeStruct((m, n), F32),
        compiler_params=_params("parallel"),
        name="in_proj",
    )(xs, nw.reshape(1, d), mods, mods, w)


def _cumsum_rows(x, reverse):
    n = x.shape[0]
    r = lax.broadcasted_iota(jnp.int32, (n, n), 0)
    c = lax.broadcasted_iota(jnp.int32, (n, n), 1)
    tri = jnp.where((c >= r) if reverse else (c <= r), 1.0, 0.0).astype(BF16)
    p1 = x.astype(BF16)
    rest = x - p1.astype(F32)
    p2 = rest.astype(BF16)
    p3 = (rest - p2.astype(F32)).astype(BF16)
    return _dot(tri, p1) + _dot(tri, p2) + _dot(tri, p3)


def _bcast_row(x, r, rows):
    return jnp.broadcast_to(x[r:r + 1, :], (rows, x.shape[1]))


def _hg_scores(qh, kh, vh, cum, st, reverse, masks):
    c = qh.shape[0]
    last = cum[0:1, :] if reverse else cum[c - 1:c, :]
    o = _dot_nt((qh * jnp.exp(cum)).astype(BF16), st.astype(BF16))
    kd = kh * jnp.exp(last - cum)
    st_new = st * jnp.exp(last) + _dot_tn(vh.astype(BF16), kd.astype(BF16))

    a = jnp.zeros((c, c), F32)
    half = c // 2
    li = 0
    while half >= HG_SUB:
        blk = 2 * half
        refs = []
        for b0 in range(0, c, blk):
            r = b0 + half if reverse else b0 + half - 1
            refs.append(_bcast_row(cum, r, blk))
        ref = refs[0] if len(refs) == 1 else jnp.concatenate(refs, axis=0)
        qs = qh * jnp.exp(cum - ref)
        ks = kh * jnp.exp(ref - cum)
        a = a + jnp.where(masks[li], _dot_nt(qs.astype(BF16), ks.astype(BF16)), 0.0)
        half //= 2
        li += 1

    pieces = []
    for g in range(c // HG_SUB):
        r0 = g * HG_SUB
        qg = qh[r0:r0 + HG_SUB, :]
        cg = cum[r0:r0 + HG_SUB, :]
        for j in range(HG_SUB):
            pieces.append(qg * jnp.exp(cg - _bcast_row(cum, r0 + j, HG_SUB)))
    rs = _dot_nt(jnp.concatenate(pieces, axis=0).astype(BF16), kh.astype(BF16))
    return o, st_new, a, rs


def _hg_finish(o, a, rs, vh, reverse):
    c = vh.shape[0]
    lane = lax.broadcasted_iota(jnp.int32, (HG_SUB, c), 1)
    rsub = lax.broadcasted_iota(jnp.int32, (HG_SUB, c), 0)
    diag = []
    for g in range(c // HG_SUB):
        acc = jnp.zeros((HG_SUB, c), F32)
        for j in range(HG_SUB):
            idx = g * HG_SUB + j
            causal = (rsub <= j) if reverse else (rsub >= j)
            acc = jnp.where((lane == idx) & causal, rs[idx * HG_SUB:(idx + 1) * HG_SUB, :], acc)
        diag.append(acc)
    a = a + jnp.concatenate(diag, axis=0)
    return o + _dot(a.astype(BF16), vh.astype(BF16))


def _hg_kernel(*refs, reverse, final, rows):
    if final:
        q_ref, f_ref, v_ref, la_ref, lb_ref, g_ref, op_ref, nw_ref, o_ref, s_ref = refs
    else:
        q_ref, f_ref, v_ref, la_ref, lb_ref, o_ref, s_ref = refs

    @pl.when(pl.program_id(1) == 0)
    def _():
        s_ref[...] = jnp.zeros_like(s_ref)

    c = HG_CHUNK
    ti = lax.broadcasted_iota(jnp.int32, (c, c), 0)
    si = lax.broadcasted_iota(jnp.int32, (c, c), 1)
    masks = []
    half = c // 2
    while half >= HG_SUB:
        blk = 2 * half
        same = (ti & -blk) == (si & -blk)
        if reverse:
            m = same & ((ti & half) == 0) & ((si & half) != 0)
        else:
            m = same & ((ti & half) != 0) & ((si & half) == 0)
        masks.append(m)
        half //= 2

    n_ch = rows // c
    order = range(n_ch - 1, -1, -1) if reverse else range(n_ch)
    gated = []
    for ci in order:
        r0 = ci * c
        z = f_ref[r0:r0 + c, :]
        ls = jnp.minimum(z, 0.0) - jnp.log(1.0 + jnp.exp(-jnp.abs(z)))
        bt = lb_ref[...] + ls
        at = jnp.broadcast_to(la_ref[...], bt.shape)
        logf = jnp.maximum(at, bt) + jnp.log(1.0 + jnp.exp(-jnp.abs(at - bt)))
        gated.append((r0, _silu(q_ref[r0:r0 + c, :]), 1.0 - jnp.exp(logf), v_ref[r0:r0 + c, :],
                      _cumsum_rows(logf, reverse)))

    states = [s_ref[hh] for hh in range(HG_HEADS)]
    heads = [slice(hh * HG_DK, (hh + 1) * HG_DK) for hh in range(HG_HEADS)]
    for r0, qf, kf, vv, cum in gated:
        partial = []
        for hh, sl in enumerate(heads):
            o, states[hh], a, rs = _hg_scores(qf[:, sl], kf[:, sl], vv[:, sl], cum[:, sl], states[hh],
                                              reverse, masks)
            partial.append((o, a, rs))
        outs = []
        for (o, a, rs), sl in zip(partial, heads):
            o = _hg_finish(o, a, rs, vv[:, sl], reverse)
            if final:
                tot = o + op_ref[r0:r0 + c, sl]
                ms = jnp.mean(tot * tot, axis=-1, keepdims=True)
                o = tot * lax.rsqrt(ms + EPS) * nw_ref[:, sl]
            outs.append(o)
        o_all = jnp.concatenate(outs, axis=1)
        if final:
            o_all = o_all * _silu(g_ref[r0:r0 + c, :])
        o_ref[r0:r0 + c, :] = o_all.astype(o_ref.dtype)
    for hh in range(HG_HEADS):
        s_ref[hh] = states[hh]


def _hg_direction(u, la, lb, batch, n_ctx, reverse, o_prev=None, nw=None):
    m = u.shape[0]
    t = m // batch
    nblk = t // TM
    ncb = n_ctx // TM
    final = o_prev is not None
    cb = OFF_HG // HG_WIDTH

    def blk(j):
        if not reverse:
            return j
        return jnp.where(j < ncb, ncb - 1 - j, nblk - 1 - (j - ncb))

    def tok(col):
        return pl.BlockSpec((TM, HG_WIDTH), lambda b, j: (b * nblk + blk(j), col))

    vec = pl.BlockSpec((1, HG_WIDTH), lambda b, j: (0, 0))
    f_col = cb + 2 if reverse else cb + 1
    in_specs = [tok(cb), tok(f_col), tok(cb + 3), vec, vec]
    args = [u, u, u, la.reshape(1, -1), lb.reshape(1, -1)]
    if final:
        in_specs += [tok(cb + 4), tok(0), vec]
        args += [u, o_prev, nw.reshape(1, -1)]
    return pl.pallas_call(
        functools.partial(_hg_kernel, reverse=reverse, final=final, rows=TM),
        grid=(batch, nblk),
        in_specs=in_specs,
        out_specs=tok(0),
        out_shape=jax.ShapeDtypeStruct((m, HG_WIDTH), BF16 if final else F32),
        scratch_shapes=[pltpu.VMEM((HG_HEADS, HG_DK, HG_DK), F32)],
        compiler_params=_params("parallel", "arbitrary"),
        name="hgrn2_bwd" if reverse else "hgrn2_fwd",
    )(*args)


def _swap_halves(x):
    lane = lax.broadcasted_iota(jnp.int32, x.shape, 1)
    return jnp.where((lane & 32) == 0, pltpu.roll(x, LANES - 32, 1), pltpu.roll(x, 32, 1))


def _head_norm_rope(x, w, cos, sin, gmat):
    x2 = x * x
    hi = x2.astype(BF16)
    lo = (x2 - hi.astype(F32)).astype(BF16)
    ms = (_dot(hi, gmat) + _dot(lo, gmat)) * (1.0 / HEAD_DIM)
    xn = x * lax.rsqrt(ms + EPS) * w
    return xn * cos + _swap_halves(xn) * sin


def _att_prep_kernel(q_ref, k_ref, v_ref, cos_ref, sin_ref, qw_ref, kw_ref, g_ref,
                     qo_ref, k0_ref, k1_ref, v0_ref, v1_ref):
    cos, sin, gmat = cos_ref[...], sin_ref[...], g_ref[...]
    qscale = (HEAD_DIM ** -0.5) * float(np.log2(np.e))
    for j in range(ATT_WIDTH // LANES):
        sl = slice(j * LANES, (j + 1) * LANES)
        qr = _head_norm_rope(q_ref[:, sl], qw_ref[...], cos, sin, gmat)
        qo_ref[:, sl] = (qr * qscale).astype(BF16)
    kr = _head_norm_rope(k_ref[...], kw_ref[...], cos, sin, gmat)
    lane = lax.broadcasted_iota(jnp.int32, kr.shape, 1)
    k0_ref[...] = jnp.where(lane < HEAD_DIM, kr, 0.0).astype(BF16)
    k1_ref[...] = jnp.where(lane >= HEAD_DIM, kr, 0.0).astype(BF16)
    v = v_ref[...]
    v0_ref[...] = jnp.where(lane < HEAD_DIM, v, 1.0).astype(BF16)
    v1_ref[...] = jnp.where(lane >= HEAD_DIM, v, 1.0).astype(BF16)


def _att_prep(u, cos, sin, qw, kw, gmat, batch):
    m = u.shape[0]
    tpb = m // batch // TM
    kvw = ATT_KV_HEADS * HEAD_DIM
    tab = pl.BlockSpec((TM, LANES), lambda i: (i % tpb, 0))
    vec = pl.BlockSpec((1, LANES), lambda i: (0, 0))
    kv_out = pl.BlockSpec((TM, kvw), lambda i: (i, 0))
    return pl.pallas_call(
        _att_prep_kernel,
        grid=(m // TM,),
        in_specs=[
            pl.BlockSpec((TM, ATT_WIDTH), lambda i: (i, OFF_ATT_Q // ATT_WIDTH)),
            pl.BlockSpec((TM, kvw), lambda i: (i, OFF_ATT_K // kvw)),
            pl.BlockSpec((TM, kvw), lambda i: (i, OFF_ATT_V // kvw)),
            tab, tab, vec, vec,
            pl.BlockSpec((LANES, LANES), lambda i: (0, 0)),
        ],
        out_specs=[pl.BlockSpec((TM, ATT_WIDTH), lambda i: (i, 0)), kv_out, kv_out, kv_out, kv_out],
        out_shape=[jax.ShapeDtypeStruct((m, ATT_WIDTH), BF16)] + [jax.ShapeDtypeStruct((m, kvw), BF16)] * 4,
        compiler_params=_params("parallel"),
        name="att_prep",
    )(u, u, u, cos, sin, qw, kw, gmat)


def _att_kernel(bound_ref, q_ref, k0_ref, k1_ref, v0_ref, v1_ref, o_ref, *, n_ctx, n_all, tq):
    bound = bound_ref[0, 0]

    def attend(r0, rows, n_keys, use_bound):
        q = q_ref[pl.ds(r0, rows), :]

        def head(k_ref, v_ref):
            s = _dot_nt(q, k_ref[0:n_keys, :])
            shift = bound if use_bound else jnp.max(s, axis=-1, keepdims=True)
            return _dot(jnp.exp2(s - shift).astype(BF16), v_ref[0:n_keys, :])

        acc0, acc1 = head(k0_ref, v0_ref), head(k1_ref, v1_ref)
        lane = lax.broadcasted_iota(jnp.int32, acc0.shape, 1)
        out = jnp.where(lane < HEAD_DIM, acc0 / pltpu.roll(acc0, HEAD_DIM, 1),
                        acc1 / pltpu.roll(acc1, HEAD_DIM, 1))
        o_ref[pl.ds(r0, rows), :] = out.astype(o_ref.dtype)

    def all_rows(use_bound):
        if n_ctx:
            attend(0, n_ctx, n_ctx, use_bound)

        def lat_chunk(i, _):
            attend(pl.multiple_of(n_ctx + i * tq, TM), tq, n_all, use_bound)
            return 0

        lax.fori_loop(0, (n_all - n_ctx) // tq, lat_chunk, 0)

    safe = bound <= ATT_SAFE_BOUND

    @pl.when(safe)
    def _():
        all_rows(True)

    @pl.when(jnp.logical_not(safe))
    def _():
        all_rows(False)


def _attention(bound, qr, k0, k1, v0, v1, batch, n_ctx):
    m = qr.shape[0]
    t = m // batch
    kvw = ATT_KV_HEADS * HEAD_DIM
    tq = ATT_TQ if (t - n_ctx) % ATT_TQ == 0 else TM
    qspec = pl.BlockSpec((t, LANES), lambda b, j: (b, j))
    kvspec = pl.BlockSpec((t, kvw), lambda b, j: (b, 0))
    return pl.pallas_call(
        functools.partial(_att_kernel, n_ctx=n_ctx, n_all=t, tq=tq),
        grid=(batch, ATT_WIDTH // LANES),
        in_specs=[pl.BlockSpec(memory_space=pltpu.SMEM), qspec, kvspec, kvspec, kvspec, kvspec],
        out_specs=qspec,
        out_shape=jax.ShapeDtypeStruct((m, ATT_WIDTH), BF16),
        compiler_params=_params("parallel", "parallel"),
        name="attention",
    )(bound, qr, k0, k1, v0, v1)


def _scan_steps(a, u, reverse, axis):
    n = a.shape[axis]
    pos = lax.broadcasted_iota(jnp.int32, a.shape, axis)
    s = 1
    while s < n:
        if reverse:
            keep = pos < n - s
            ash, ush = pltpu.roll(a, n - s, axis), pltpu.roll(u, n - s, axis)
        else:
            keep = pos >= s
            ash, ush = pltpu.roll(a, s, axis), pltpu.roll(u, s, axis)
        u = u + a * jnp.where(keep, ush, 0.0)
        a = a * jnp.where(keep, ash, 1.0)
        s *= 2
    return a, u


def _scan_rows(a, u, carry, reverse, sa_ref, su_ref):
    n, width = a.shape
    groups = n // SUBLANES
    a3, u3 = _scan_steps(a.reshape(groups, SUBLANES, width), u.reshape(groups, SUBLANES, width), reverse, 1)
    a, u = a3.reshape(n, width), u3.reshape(n, width)
    sa_ref[...] = a
    su_ref[...] = u
    edge = 0 if reverse else SUBLANES - 1
    ae = sa_ref[pl.ds(edge, groups, stride=SUBLANES), :]
    ue = su_ref[pl.ds(edge, groups, stride=SUBLANES), :]
    ae, ue = _scan_steps(ae, ue, reverse, 0)
    hc = ue + ae * carry
    grow = lax.broadcasted_iota(jnp.int32, hc.shape, 0)
    if reverse:
        cin = jnp.where(grow < groups - 1, pltpu.roll(hc, groups - 1, 0), carry)
        out = hc[0:1, :]
    else:
        cin = jnp.where(grow >= 1, pltpu.roll(hc, 1, 0), carry)
        out = hc[groups - 1:groups, :]
    cin_rows = jnp.concatenate([_bcast_row(cin, i, SUBLANES) for i in range(groups)], axis=0)
    return u + a * cin_rows, out


def _lru_kernel(x_ref, g_ref, cw_ref, cb_ref, w_ref, b_ref, lam_ref, o_ref, xp_ref, xc_ref, h_ref,
                sa_ref, su_ref, *, n_ctx, n_tok):
    tt = TM
    pad = SUBLANES
    n_lat = n_tok - n_ctx
    lat0 = n_ctx + 2 * pad
    zeros = jnp.zeros((pad, LANES), F32)
    xp_ref[0:pad, :] = zeros
    xp_ref[pad + n_ctx:lat0, :] = zeros
    xp_ref[lat0 + n_lat:lat0 + n_lat + pad, :] = zeros
    xp_ref[pad:pad + n_ctx, :] = x_ref[0:n_ctx, :]
    xp_ref[lat0:lat0 + n_lat, :] = x_ref[n_ctx:n_tok, :]

    n_tiles = n_tok // tt
    ctx_tiles = n_ctx // tt

    def conv_tile(i, _):
        r0 = pl.multiple_of(i * tt, tt)
        p0 = pl.multiple_of(jnp.where(i < ctx_tiles, r0, r0 + pad), pad)
        ext = xp_ref[pl.ds(p0, tt + 2 * pad), :]
        acc = cb_ref[...] + cw_ref[0:1, :] * ext[pad - 2:pad - 2 + tt, :]
        for j in range(1, CONV_W):
            acc = acc + cw_ref[j:j + 1, :] * ext[pad - 2 + j:pad - 2 + j + tt, :]
        xc_ref[pl.ds(r0, tt), :] = acc
        return 0

    lax.fori_loop(0, n_tiles, conv_tile, 0)

    lam = lam_ref[...]
    sp = jnp.maximum(-lam, 0.0) + jnp.log1p(jnp.exp(-jnp.abs(lam)))

    def coeffs(r0, d):
        xc = xc_ref[pl.ds(r0, tt), :]
        pre = _dot(xc.astype(BF16), w_ref[:, d * 2 * LANES:(d + 1) * 2 * LANES]) \
            + b_ref[:, d * 2 * LANES:(d + 1) * 2 * LANES]
        r = _sigmoid(pre[:, :LANES])
        ig = _sigmoid(pre[:, LANES:])
        log_a = (-RG_C) * r * sp[d:d + 1, :]
        a = jnp.exp(log_a)
        t = 1.0 - a * a
        root = jnp.where(t > 0.0, t * lax.rsqrt(t), 0.0)
        return a, root * (ig * xc)

    def scan_tile(j, carry):
        cf, cb = carry
        rf = pl.multiple_of(j * tt, tt)
        ib = jnp.where(j < ctx_tiles, ctx_tiles - 1 - j, n_tiles - 1 - (j - ctx_tiles))
        rb = pl.multiple_of(ib * tt, tt)
        af, uf = coeffs(rf, 0)
        ab, ub = coeffs(rb, 1)
        hf, cf = _scan_rows(af, uf, cf, False, sa_ref.at[0], su_ref.at[0])
        hb, cb = _scan_rows(ab, ub, cb, True, sa_ref.at[1], su_ref.at[1])
        h_ref[0, pl.ds(rf, tt), :] = hf
        h_ref[1, pl.ds(rb, tt), :] = hb
        return cf, cb

    zero = jnp.zeros((1, LANES), F32)
    lax.fori_loop(0, n_tiles, scan_tile, (zero, zero))

    def gate_tile(i, _):
        r0 = pl.multiple_of(i * tt, tt)
        tot = h_ref[0, pl.ds(r0, tt), :] + h_ref[1, pl.ds(r0, tt), :]
        o_ref[pl.ds(r0, tt), :] = (tot * jax.nn.gelu(g_ref[pl.ds(r0, tt), :])).astype(o_ref.dtype)
        return 0

    lax.fori_loop(0, n_tiles, gate_tile, 0)


def _rglru(u, conv_w, conv_b, wcat, bcat, lam, batch, n_ctx):
    m = u.shape[0]
    t = m // batch
    ncg = LRU_WIDTH // LANES
    xcol = OFF_LRU // LANES
    return pl.pallas_call(
        functools.partial(_lru_kernel, n_ctx=n_ctx, n_tok=t),
        grid=(batch, ncg),
        in_specs=[
            pl.BlockSpec((t, LANES), lambda b, c: (b, xcol + c)),
            pl.BlockSpec((t, LANES), lambda b, c: (b, xcol + ncg + c)),
            pl.BlockSpec((CONV_W, LANES), lambda b, c: (0, c)),
            pl.BlockSpec((1, LANES), lambda b, c: (0, c)),
            pl.BlockSpec((None, LANES, 4 * LANES), lambda b, c: (c, 0, 0)),
            pl.BlockSpec((None, 1, 4 * LANES), lambda b, c: (c, 0, 0)),
            pl.BlockSpec((2, LANES), lambda b, c: (0, c)),
        ],
        out_specs=pl.BlockSpec((t, LANES), lambda b, c: (b, c)),
        out_shape=jax.ShapeDtypeStruct((m, LRU_WIDTH), BF16),
        scratch_shapes=[pltpu.VMEM((t + 3 * SUBLANES, LANES), F32),
                        pltpu.VMEM((t, LANES), F32),
                        pltpu.VMEM((2, t, LANES), F32),
                        pltpu.VMEM((2, TM, LANES), F32),
                        pltpu.VMEM((2, TM, LANES), F32)],
        compiler_params=_params("parallel", "parallel"),
        name="rglru",
    )(u, u, conv_w, conv_b.reshape(1, -1), wcat, bcat, lam)


def _merge_kernel(ya_ref, yb_ref, yc_ref, ml_ref, xs_ref, g1_ref, wa_ref, wb_ref, wc_ref, wo_ref, o_ref):
    d = D_MODEL
    acc = _sigmoid(ml_ref[:, 0:d]) * _dot(ya_ref[...], wa_ref[...])
    acc = acc + _sigmoid(ml_ref[:, d:2 * d]) * _dot(yb_ref[...], wb_ref[...])
    acc = acc + _sigmoid(ml_ref[:, 2 * d:3 * d]) * _dot(yc_ref[...], wc_ref[...])
    y = _dot(acc.astype(BF16), wo_ref[...])
    o_ref[...] = xs_ref[...] + g1_ref[...] * y


def _merge(ya, yb, yc, u, xs, mods, wa, wb, wc, wo, batch, n_ctx, drop_ctx):
    m, d = xs.shape
    tpb = m // batch // TM
    ctx_tiles = n_ctx // TM
    if drop_ctx:
        out_tpb = tpb - ctx_tiles
        src = lambda i: (i // out_tpb) * tpb + ctx_tiles + i % out_tpb
        row = lambda i: i // out_tpb
        m_out = batch * out_tpb * TM
    else:
        src = lambda i: i
        row = functools.partial(_mod_row, tiles_per_batch=tpb, ctx_tiles=ctx_tiles)
        m_out = m
    br = pl.BlockSpec((TM, HG_WIDTH), lambda i: (src(i), 0))
    full = lambda a: pl.BlockSpec(a.shape, lambda i: (0, 0))
    return pl.pallas_call(
        _merge_kernel,
        grid=(m_out // TM,),
        in_specs=[br, br, br,
                  pl.BlockSpec((TM, 3 * d), lambda i: (src(i), OFF_MERGE)),
                  pl.BlockSpec((TM, d), lambda i: (src(i), 0)),
                  pl.BlockSpec((None, 1, d), lambda i: (row(i), 0, 2)),
                  full(wa), full(wb), full(wc), full(wo)],
        out_specs=pl.BlockSpec((TM, d), lambda i: (i, 0)),
        out_shape=jax.ShapeDtypeStruct((m_out, d), F32),
        compiler_params=_params("parallel"),
        name="merge_out",
    )(ya, yb, yc, u, xs, mods, wa, wb, wc, wo)


def _swiglu_acc(hb, wg_ref, wu_ref, wd_ref):
    acc = jnp.zeros((hb.shape[0], wd_ref.shape[1]), F32)
    for c0 in range(0, D_FF, FF_CHUNK):
        a = _dot(hb, wg_ref[:, c0:c0 + FF_CHUNK])
        b = _dot(hb, wu_ref[:, c0:c0 + FF_CHUNK])
        acc = acc + _dot((_silu(a) * b).astype(BF16), wd_ref[c0:c0 + FF_CHUNK, :])
    return acc


def _final_norm(y, fw):
    ms = jnp.mean(y * y, axis=-1, keepdims=True)
    return y * lax.rsqrt(ms + EPS) * fw


def _ffn_kernel(x_ref, nw_ref, sc_ref, sh_ref, g2_ref, wg_ref, wu_ref, wd_ref, fw_ref, o_ref, *, final):
    x = x_ref[...]
    hb = _norm_mod(x, nw_ref[...], sc_ref[...], sh_ref[...]).astype(BF16)
    y = x + g2_ref[...] * _swiglu_acc(hb, wg_ref, wu_ref, wd_ref)
    o_ref[...] = _final_norm(y, fw_ref[...]) if final else y


def _mod_specs(row, d):
    return [pl.BlockSpec((None, 1, d), lambda i, *_: (row(i), 0, 4)),
            pl.BlockSpec((None, 1, d), lambda i, *_: (row(i), 0, 3)),
            pl.BlockSpec((None, 1, d), lambda i, *_: (row(i), 0, 5))]


def _ffn(xs, nw, mods, wg, wu, wd, fw, tiles_per_batch, ctx_tiles, final):
    m, d = xs.shape
    row = functools.partial(_mod_row, tiles_per_batch=tiles_per_batch, ctx_tiles=ctx_tiles)
    full = lambda a: pl.BlockSpec(a.shape, lambda i: (0, 0))
    vec = pl.BlockSpec((1, d), lambda i: (0, 0))
    return pl.pallas_call(
        functools.partial(_ffn_kernel, final=final),
        grid=(m // TM,),
        in_specs=[pl.BlockSpec((TM, d), lambda i: (i, 0)), vec] + _mod_specs(row, d)
                 + [full(wg), full(wu), full(wd), vec],
        out_specs=pl.BlockSpec((TM, d), lambda i: (i, 0)),
        out_shape=jax.ShapeDtypeStruct((m, d), F32),
        compiler_params=_params("parallel"),
        name="ffn_dense",
    )(xs, nw.reshape(1, d), mods, mods, mods, wg, wu, wd, fw.reshape(1, d))


def _top2(logits):
    lane = lax.broadcasted_iota(jnp.int32, logits.shape, 1)
    m1 = jnp.max(logits, axis=-1, keepdims=True)
    i1 = jnp.min(jnp.where(logits == m1, lane, LANES), axis=-1, keepdims=True)
    rest = jnp.where(lane == i1, -jnp.inf, logits)
    m2 = jnp.max(rest, axis=-1, keepdims=True)
    i2 = jnp.min(jnp.where(rest == m2, lane, LANES), axis=-1, keepdims=True)
    e2 = jnp.exp(m2 - m1)
    w1 = 1.0 / (1.0 + e2)
    return i1, i2, w1, e2 * w1


def _router_kernel(x_ref, nw_ref, sc_ref, sh_ref, rw_ref, meta_ref, wts_ref, cnt_ref, base_ref):
    @pl.when(pl.program_id(0) == 0)
    def _():
        base_ref[...] = jnp.zeros_like(base_ref)

    h = _norm_mod(x_ref[...], nw_ref[...], sc_ref[...], sh_ref[...])
    lane = lax.broadcasted_iota(jnp.int32, (h.shape[0], LANES), 1)
    logits = jnp.full((h.shape[0], LANES), -jnp.inf, F32)
    for e in range(N_EXPERTS):
        logits = jnp.where(lane == e, jnp.sum(h * rw_ref[e:e + 1, :], axis=-1, keepdims=True), logits)
    i1, i2, w1, w2 = _top2(logits)
    hit = jnp.where((lane == i1) | (lane == i2), 1.0, 0.0)
    tm = hit.shape[0]
    tri = lax.broadcasted_iota(jnp.int32, (tm, tm), 0) > lax.broadcasted_iota(jnp.int32, (tm, tm), 1)
    before = _dot(jnp.where(tri, 1.0, 0.0).astype(BF16), hit.astype(BF16)) + base_ref[...]
    r1 = jnp.sum(jnp.where(lane == i1, before, 0.0), axis=-1, keepdims=True).astype(jnp.int32)
    r2 = jnp.sum(jnp.where(lane == i2, before, 0.0), axis=-1, keepdims=True).astype(jnp.int32)
    base_ref[...] += jnp.sum(hit, axis=0, keepdims=True)
    cnt_ref[...] = base_ref[...]
    meta_ref[...] = jnp.where(lane == 0, i1, jnp.where(lane == 1, i2, jnp.where(lane == 2, r1,
                              jnp.where(lane == 3, r2, 0))))
    wts_ref[...] = jnp.where(lane == 0, w1, jnp.where(lane == 1, w2, 0.0))


def _router(xs, nw, mods, rw, row):
    m, d = xs.shape
    vec = pl.BlockSpec((1, d), lambda i: (0, 0))
    tok = pl.BlockSpec((TM, LANES), lambda i: (i, 0))
    return pl.pallas_call(
        _router_kernel,
        grid=(m // TM,),
        in_specs=[pl.BlockSpec((TM, d), lambda i: (i, 0)), vec] + _mod_specs(row, d)[:2]
                 + [pl.BlockSpec((N_EXPERTS, d), lambda i: (0, 0))],
        out_specs=[tok, tok, pl.BlockSpec((1, LANES), lambda i: (0, 0))],
        out_shape=[jax.ShapeDtypeStruct((m, LANES), jnp.int32), jax.ShapeDtypeStruct((m, LANES), F32),
                   jax.ShapeDtypeStruct((1, LANES), F32)],
        scratch_shapes=[pltpu.VMEM((1, LANES), F32)],
        compiler_params=_params("arbitrary"),
        name="moe_router",
    )(xs, nw.reshape(1, d), mods, mods, rw)


def _row_copy(src_ref, s, dst_ref, t, sem):
    return pltpu.make_async_copy(src_ref.at[pl.ds(s, 1)], dst_ref.at[pl.ds(t, 1)], sem)


def _dispatch_kernel(pos_ref, x_ref, nw_ref, sc_ref, sh_ref, xin_ref, xbuf_ref, h_ref, sem):
    del xin_ref
    h_ref[...] = _norm_mod(x_ref[...], nw_ref[...], sc_ref[...], sh_ref[...])
    tm = h_ref.shape[0]

    def issue(r, _):
        _row_copy(h_ref, r, xbuf_ref, pos_ref[0, 2 * r], sem).start(priority=0)
        _row_copy(h_ref, r, xbuf_ref, pos_ref[0, 2 * r + 1], sem).start(priority=1)
        return 0

    lax.fori_loop(0, tm, issue, 0, unroll=DMA_UNROLL)

    def drain(r, _):
        _row_copy(h_ref, 0, xbuf_ref, 0, sem).wait()
        _row_copy(h_ref, 0, xbuf_ref, 0, sem).wait()
        return 0

    lax.fori_loop(0, tm, drain, 0, unroll=DMA_UNROLL)


def _dispatch(xs, nw, mods, pos, row, n_rows):
    m, d = xs.shape
    vec = pl.BlockSpec((1, d), lambda i: (0, 0))
    slab = (d,)
    zeros = jnp.zeros((n_rows,) + slab, F32)
    return pl.pallas_call(
        _dispatch_kernel,
        grid=(m // TM,),
        in_specs=[pl.BlockSpec((None, 1, 2 * TM), lambda i: (i, 0, 0), memory_space=pltpu.SMEM),
                  pl.BlockSpec((TM, d), lambda i: (i, 0)), vec] + _mod_specs(row, d)[:2]
                 + [pl.BlockSpec(memory_space=pl.ANY)],
        out_specs=pl.BlockSpec(memory_space=pl.ANY),
        out_shape=jax.ShapeDtypeStruct((n_rows,) + slab, F32),
        scratch_shapes=[pltpu.VMEM((TM,) + slab, F32), pltpu.SemaphoreType.DMA(())],
        input_output_aliases={5: 0},
        compiler_params=_params("arbitrary"),
        name="moe_dispatch",
    )(pos, xs, nw.reshape(1, d), mods, mods, zeros)


def _expert_ffn_kernel(te_ref, nu_ref, x_ref, wg_ref, wu_ref, wd_ref, y_ref):
    del te_ref
    used = pl.program_id(0) < nu_ref[0]

    @pl.when(used)
    def _():
        y_ref[...] = _swiglu_acc(x_ref[...].astype(BF16), wg_ref, wu_ref, wd_ref)

    @pl.when(jnp.logical_not(used))
    def _():
        y_ref[...] = jnp.zeros_like(y_ref)


def _expert_ffn(xbuf, tile_expert, n_used, wg, wu, wd):
    n_rows = xbuf.shape[0]
    d = wg.shape[1]
    wspec = lambda shape: pl.BlockSpec((None,) + shape, lambda i, te, nu: (te[i], 0, 0))
    tok = pl.BlockSpec((TM, d), lambda i, te, nu: (i, 0))
    return pl.pallas_call(
        _expert_ffn_kernel,
        grid_spec=pltpu.PrefetchScalarGridSpec(
            num_scalar_prefetch=2,
            grid=(n_rows // TM,),
            in_specs=[tok, wspec((d, D_FF)), wspec((d, D_FF)), wspec((D_FF, d))],
            out_specs=tok,
        ),
        out_shape=jax.ShapeDtypeStruct(xbuf.shape, F32),
        compiler_params=_params("arbitrary"),
        name="moe_expert_ffn",
    )(tile_expert, n_used, xbuf, wg, wu, wd)


def _combine_kernel(pos_ref, x_ref, wts_ref, g2_ref, fw_ref, y_ref, o_ref, buf_ref, sem, *, final):
    tm = x_ref.shape[0]

    def issue(r, _):
        _row_copy(y_ref, pos_ref[0, 2 * r], buf_ref.at[0], r, sem).start(priority=0)
        _row_copy(y_ref, pos_ref[0, 2 * r + 1], buf_ref.at[1], r, sem).start(priority=1)
        return 0

    lax.fori_loop(0, tm, issue, 0, unroll=DMA_UNROLL)

    def drain(r, _):
        _row_copy(y_ref, 0, buf_ref.at[0], 0, sem).wait()
        _row_copy(y_ref, 0, buf_ref.at[0], 0, sem).wait()
        return 0

    lax.fori_loop(0, tm, drain, 0, unroll=DMA_UNROLL)
    w = wts_ref[...]
    f = w[:, 0:1] * buf_ref[0] + w[:, 1:2] * buf_ref[1]
    y = x_ref[...] + g2_ref[...] * f
    o_ref[...] = _final_norm(y, fw_ref[...]) if final else y


def _combine(xs, wts, mods, fw, ybuf, pos, row, final):
    m, d = xs.shape
    vec = pl.BlockSpec((1, d), lambda i: (0, 0))
    return pl.pallas_call(
        functools.partial(_combine_kernel, final=final),
        grid=(m // TM,),
        in_specs=[pl.BlockSpec((None, 1, 2 * TM), lambda i: (i, 0, 0), memory_space=pltpu.SMEM),
                  pl.BlockSpec((TM, d), lambda i: (i, 0)),
                  pl.BlockSpec((TM, LANES), lambda i: (i, 0)),
                  _mod_specs(row, d)[2], vec,
                  pl.BlockSpec(memory_space=pl.ANY)],
        out_specs=pl.BlockSpec((TM, d), lambda i: (i, 0)),
        out_shape=jax.ShapeDtypeStruct((m, d), F32),
        scratch_shapes=[pltpu.VMEM((2, TM) + ybuf.shape[1:], F32), pltpu.SemaphoreType.DMA(())],
        compiler_params=_params("arbitrary"),
        name="moe_combine",
    )(pos, xs, wts, mods, fw.reshape(1, d), ybuf)


def _moe(xs, nw, mods, rw, wg, wu, wd, fw, tiles_per_batch, ctx_tiles, final):
    m, d = xs.shape
    row = functools.partial(_mod_row, tiles_per_batch=tiles_per_batch, ctx_tiles=ctx_tiles)
    meta, wts, cnt = _router(xs, nw, mods, rw, row)
    counts = cnt[0, :N_EXPERTS].astype(jnp.int32)
    padded = (counts + TM - 1) // TM * TM
    ends = jnp.cumsum(padded)
    starts = ends - padded
    pos = jnp.stack([starts[meta[:, 0]] + meta[:, 2], starts[meta[:, 1]] + meta[:, 3]], axis=-1)
    pos = pos.reshape(m // TM, 1, 2 * TM)
    n_rows = 2 * m + N_EXPERTS * TM
    n_used = (ends[-1] // TM).astype(jnp.int32)
    tile_start = jnp.arange(n_rows // TM, dtype=jnp.int32) * TM
    tile_row = jnp.minimum(tile_start, ends[-1] - 1)
    tile_expert = jnp.sum((tile_row[:, None] >= ends[None, :]).astype(jnp.int32), axis=1)
    xbuf = _dispatch(xs, nw, mods, pos, row, n_rows)
    ybuf = _expert_ffn(xbuf, tile_expert, n_used.reshape(1), wg, wu, wd)
    return _combine(xs, wts, mods, fw, ybuf, pos, row, final)


def _cast_kernel(x_ref, o_ref):
    o_ref[...] = x_ref[...].astype(o_ref.dtype)


def _to_bf16(w, layer):
    cols = w.shape[-1]
    w2 = w.reshape(-1, cols)
    blocks = w2.shape[0] // w.shape[0] // TM
    out = pl.pallas_call(
        _cast_kernel,
        grid=(blocks,),
        in_specs=[pl.BlockSpec((TM, cols), lambda i: (layer * blocks + i, 0))],
        out_specs=pl.BlockSpec((TM, cols), lambda i: (i, 0)),
        out_shape=jax.ShapeDtypeStruct((blocks * TM, cols), BF16),
        compiler_params=_params("parallel"),
        name="weights_to_bf16",
    )(w2)
    return out.reshape(w.shape[1:])


def _column_perm():
    sizes = (512, 512, 512, 512, 512, 512, 128, 128, 512, 512, 3 * D_MODEL)
    starts = np.concatenate([[0], np.cumsum(sizes)[:-1]])
    a_q, a_ff, a_fb, a_v, a_g, b_q, b_k, b_v, c_x, c_g, mrg = [
        np.arange(s, s + n) for s, n in zip(starts, sizes)]
    half = np.concatenate([np.arange(0, HEAD_DIM, 2), np.arange(1, HEAD_DIM, 2)])
    group = ATT_HEADS // ATT_KV_HEADS
    q_cols = []
    for j in range(group):
        q_cols += [b_q[j * HEAD_DIM + half], b_q[(group + j) * HEAD_DIM + half]]
    k_cols = [b_k[h * HEAD_DIM + half] for h in range(ATT_KV_HEADS)]
    return np.concatenate([mrg, a_q, a_ff, a_fb, a_v, a_g, c_x, c_g] + q_cols + k_cols + [b_v])


def _permute_columns(w, perm):
    cuts = np.flatnonzero(np.diff(perm) != 1) + 1
    parts, pending = [], []
    for run in np.split(perm, cuts):
        if len(run) >= LANES:
            if pending:
                parts.append(w[:, np.concatenate(pending)])
                pending = []
            parts.append(w[:, int(run[0]):int(run[-1]) + 1])
        else:
            pending.append(run)
    if pending:
        parts.append(w[:, np.concatenate(pending)])
    return jnp.concatenate(parts, axis=1)


def _att_out_rows():
    group = ATT_HEADS // ATT_KV_HEADS
    idx = []
    for j in range(group):
        idx += [np.arange(j * HEAD_DIM, (j + 1) * HEAD_DIM),
                np.arange((group + j) * HEAD_DIM, (group + j + 1) * HEAD_DIM)]
    return np.concatenate(idx)


def _rope_tables(n_ctx, n_lat):
    rows = n_lat // GRID_W
    row = jnp.repeat(jnp.arange(rows, dtype=F32), GRID_W)
    col = jnp.tile(jnp.arange(GRID_W, dtype=F32), rows)
    pairs = HEAD_DIM // 4
    freqs = ROPE_THETA ** (-jnp.arange(pairs, dtype=F32) / pairs)
    ang = jnp.concatenate([row[:, None] * freqs, col[:, None] * freqs], axis=-1)
    cos, sin = jnp.cos(ang), jnp.sin(ang)
    cos = jnp.concatenate([jnp.ones((n_ctx, HEAD_DIM // 2), F32), cos], axis=0)
    sin = jnp.concatenate([jnp.zeros((n_ctx, HEAD_DIM // 2), F32), sin], axis=0)
    reps = LANES // HEAD_DIM
    return (jnp.tile(jnp.concatenate([cos, cos], axis=-1), (1, reps)),
            jnp.tile(jnp.concatenate([-sin, sin], axis=-1), (1, reps)))


def _lru_weights(wa, wx, ba, bx):
    ncg = LRU_WIDTH // LANES

    def dense(w):
        w = w.reshape(ncg, 2, LRU_BLOCK_W, LRU_BLOCK_W)
        out = jnp.zeros((ncg, LANES, LANES), w.dtype)
        out = out.at[:, :LRU_BLOCK_W, :LRU_BLOCK_W].set(w[:, 0])
        return out.at[:, LRU_BLOCK_W:, LRU_BLOCK_W:].set(w[:, 1])

    wcat = jnp.concatenate([dense(wa[0]), dense(wx[0]), dense(wa[1]), dense(wx[1])], axis=-1).astype(BF16)
    bcat = jnp.concatenate([b.reshape(ncg, 1, LANES) for b in (ba[0], bx[0], ba[1], bx[1])], axis=-1)
    return wcat, bcat


def kernel(x, c, ctx, c_ctx, ada_w, ada_b, mix_norm_w, ffn_norm_w, w_in, hg_lb_logits, hg_norm_w, q_norm_w, k_norm_w, lru_conv_w, lru_conv_b, lru_wa, lru_ba, lru_wx, lru_bx, lru_lambda, w_br_a, w_br_b, w_br_c, w_out, ffn_w_gate, ffn_w_up, ffn_w_down, router_w, moe_w_gate, moe_w_up, moe_w_down, final_norm_w):
    batch, n_lat, d = x.shape
    n_ctx = ctx.shape[1]
    t = n_ctx + n_lat
    depth = ada_w.shape[0]
    assert n_ctx % TM == 0 and n_lat % TM == 0 and batch <= 4

    cvec = jnp.zeros((SUBLANES, d), F32).at[:batch].set(c).at[4].set(c_ctx)
    mods_all = _ada_mods(cvec, ada_w, ada_b).reshape(depth, SUBLANES, 1, 6 * d)

    p = jax.nn.softmax(hg_lb_logits.astype(F32), axis=0)
    cum = jnp.cumsum(p, axis=0)
    lbs = cum - cum[:1]
    log_lb, log_1m_lb = jnp.log(lbs), jnp.log1p(-lbs)

    cos, sin = _rope_tables(n_ctx, n_lat)
    half = np.concatenate([np.arange(0, HEAD_DIM, 2), np.arange(1, HEAD_DIM, 2)])
    gidx = np.arange(LANES) // HEAD_DIM
    gmat = jnp.asarray(gidx[:, None] == gidx[None, :], BF16)
    col_perm = _column_perm()
    att_rows = _att_out_rows()

    xs = jnp.concatenate([ctx.astype(x.dtype), x], axis=1).reshape(batch * t, d)
    tpb, ctx_tiles = t // TM, n_ctx // TM
    for i in range(depth):
        last = i == depth - 1
        mods = mods_all[i]
        w_i = _permute_columns(w_in[i], col_perm).astype(BF16)
        u = _in_proj(xs, mix_norm_w[i], mods, w_i, tpb, ctx_tiles)

        o_f = _hg_direction(u, log_lb[i, 0], log_1m_lb[i, 0], batch, n_ctx, False)
        y_a = _hg_direction(u, log_lb[i, 1], log_1m_lb[i, 1], batch, n_ctx, True, o_f, hg_norm_w[i])

        reps = LANES // HEAD_DIM
        qw = jnp.tile(q_norm_w[i][half], reps).reshape(1, LANES)
        kw = jnp.tile(k_norm_w[i][half], reps).reshape(1, LANES)
        qr, k0, k1, v0, v1 = _att_prep(u, cos, sin, qw, kw, gmat, batch)
        bound = (1.02 * HEAD_DIM ** 0.5 * float(np.log2(np.e))
                 * jnp.max(jnp.abs(q_norm_w[i])) * jnp.max(jnp.abs(k_norm_w[i]))).reshape(1, 1)
        y_b = _attention(bound, qr, k0, k1, v0, v1, batch, n_ctx)

        wcat, bcat = _lru_weights(lru_wa[i], lru_wx[i], lru_ba[i], lru_bx[i])
        y_c = _rglru(u, lru_conv_w[i], lru_conv_b[i], wcat, bcat, lru_lambda[i], batch, n_ctx)

        xs = _merge(y_a, y_b, y_c, u, xs, mods, w_br_a[i].astype(BF16), w_br_b[i][att_rows].astype(BF16),
                    w_br_c[i].astype(BF16), w_out[i].astype(BF16), batch, n_ctx, last)
        if last:
            tpb, ctx_tiles = n_lat // TM, 0
        if i % 2 == 0:
            j = i // 2
            xs = _ffn(xs, ffn_norm_w[i], mods, _to_bf16(ffn_w_gate, j), _to_bf16(ffn_w_up, j),
                      _to_bf16(ffn_w_down, j), final_norm_w, tpb, ctx_tiles, last)
        else:
            j = i // 2
            rw = router_w[j].T
            xs = _moe(xs, ffn_norm_w[i], mods, rw, _to_bf16(moe_w_gate, j), _to_bf16(moe_w_up, j),
                      _to_bf16(moe_w_down, j), final_norm_w, tpb, ctx_tiles, last)
    return xs.reshape(batch, n_lat, d)
```

```python
import functools

import numpy as np
import jax
import jax.numpy as jnp
from jax import lax
from jax.experimental import pallas as pl
from jax.experimental.pallas import tpu as pltpu

F32 = jnp.float32
BF16 = jnp.bfloat16

D_MODEL = 1024
DEPTH = 4
GRID_W = 64
HG_HEADS = 4
HG_DK = 128
HG_WIDTH = 512
ATT_HEADS = 8
ATT_KV_HEADS = 2
HEAD_DIM = 64
ATT_WIDTH = 512
ROPE_THETA = 10000.0
LRU_WIDTH = 512
LRU_BLOCK_W = 64
CONV_W = 4
RG_C = 8.0
D_FF = 2816
N_EXPERTS = 8
EPS = 1e-6

LOG2E = float(np.log2(np.e))
LANES = 128
SUBLANES = 8
TM = 256
HG_CHUNK = 64
HG_SUB = 8
ATT_TQ = 512
ATT_SAFE_BOUND = 60.0
ATT_VT_ROWS = 80
FF_CHUNK = 1408
DMA_UNROLL = 8
VMEM_LIMIT = 56 * 1024 * 1024

OFF_MERGE = 0
OFF_HG = 3 * D_MODEL
OFF_LRU = OFF_HG + 5 * HG_WIDTH
OFF_ATT_Q = OFF_LRU + 2 * LRU_WIDTH
OFF_ATT_K = OFF_ATT_Q + ATT_WIDTH
OFF_ATT_V = OFF_ATT_K + ATT_KV_HEADS * HEAD_DIM
IN_DIM = OFF_ATT_V + ATT_KV_HEADS * HEAD_DIM


def _dot(a, b):
    return jnp.dot(a, b, preferred_element_type=F32)


def _dot_nt(a, b):
    return lax.dot_general(a, b, (((1,), (1,)), ((), ())), preferred_element_type=F32)


def _dot_tn(a, b):
    return lax.dot_general(a, b, (((0,), (0,)), ((), ())), preferred_element_type=F32)


def _sigmoid(x):
    return 1.0 / (1.0 + jnp.exp(-x))


def _silu(x):
    return x * _sigmoid(x)


def _params(*sem):
    return pltpu.CompilerParams(dimension_semantics=sem, vmem_limit_bytes=VMEM_LIMIT)


def _mod_row(i, tiles_per_batch, ctx_tiles):
    if ctx_tiles == 0:
        return i // tiles_per_batch
    return jnp.where(i % tiles_per_batch < ctx_tiles, 4, i // tiles_per_batch)


def _norm_mod(x, nw, sc, sh):
    ms = jnp.mean(x * x, axis=-1, keepdims=True)
    return (x * lax.rsqrt(ms + EPS) * nw) * (1.0 + sc) + sh


def _ada_kernel(c_ref, w_ref, b_ref, o_ref):
    s = _silu(c_ref[...])
    o_ref[...] = _dot(s.astype(BF16), w_ref[...].astype(BF16)) + b_ref[...]


def _ada_mods(cvec, ada_w, ada_b):
    depth, d, n = ada_w.shape
    tn = 1536
    return pl.pallas_call(
        _ada_kernel,
        grid=(depth, n // tn),
        in_specs=[
            pl.BlockSpec((SUBLANES, d), lambda l, j: (0, 0)),
            pl.BlockSpec((None, d, tn), lambda l, j: (l, 0, j)),
            pl.BlockSpec((None, 1, tn), lambda l, j: (l, 0, j)),
        ],
        out_specs=pl.BlockSpec((None, SUBLANES, tn), lambda l, j: (l, 0, j)),
        out_shape=jax.ShapeDtypeStruct((depth, SUBLANES, n), F32),
        compiler_params=_params("parallel", "parallel"),
        name="ada_mods",
    )(cvec, ada_w, ada_b.reshape(depth, 1, n))


def _in_proj_kernel(x_ref, nw_ref, sc_ref, sh_ref, w_ref, o_ref):
    hb = _norm_mod(x_ref[...], nw_ref[...], sc_ref[...], sh_ref[...]).astype(BF16)
    n = w_ref.shape[1]
    step = 512
    for c0 in range(0, n, step):
        cw = min(step, n - c0)
        o_ref[:, c0:c0 + cw] = _dot(hb, w_ref[:, c0:c0 + cw]).astype(o_ref.dtype)


def _in_proj(xs, nw, mods, w, tiles_per_batch, ctx_tiles):
    m, d = xs.shape
    n = w.shape[1]
    row = functools.partial(_mod_row, tiles_per_batch=tiles_per_batch, ctx_tiles=ctx_tiles)
    return pl.pallas_call(
        _in_proj_kernel,
        grid=(m // TM,),
        in_specs=[
            pl.BlockSpec((TM, d), lambda i: (i, 0)),
            pl.BlockSpec((1, d), lambda i: (0, 0)),
            pl.BlockSpec((None, 1, d), lambda i: (row(i), 0, 1)),
            pl.BlockSpec((None, 1, d), lambda i: (row(i), 0, 0)),
            pl.BlockSpec((d, n), lambda i: (0, 0)),
        ],
        out_specs=pl.BlockSpec((TM, n), lambda i: (i, 0)),
        out_shape=jax.ShapeDtypeStruct((m, n), F32),
        compiler_params=_params("parallel"),
        name="in_proj",
    )(xs, nw.reshape(1, d), mods, mods, w)


def _cumsum_rows(x, reverse):
    n = x.shape[0]
    r = lax.broadcasted_iota(jnp.int32, (n, n), 0)
    c = lax.broadcasted_iota(jnp.int32, (n, n), 1)
    tri = jnp.where((c >= r) if reverse else (c <= r), 1.0, 0.0).astype(BF16)
    p1 = x.astype(BF16)
    rest = x - p1.astype(F32)
    p2 = rest.astype(BF16)
    p3 = (rest - p2.astype(F32)).astype(BF16)
    return _dot(tri, p1) + _dot(tri, p2) + _dot(tri, p3)


def _bcast_row(x, r, rows):
    return jnp.broadcast_to(x[r:r + 1, :], (rows, x.shape[1]))


def _hg_scores(qh, kh, vh, cum, st, reverse, masks):
    c = qh.shape[0]
    last = cum[0:1, :] if reverse else cum[c - 1:c, :]
    o = _dot_nt((qh * jnp.exp2(cum)).astype(BF16), st.astype(BF16))
    kd = kh * jnp.exp2(last - cum)
    st_new = st * jnp.exp2(last) + _dot_tn(vh.astype(BF16), kd.astype(BF16))

    a = jnp.zeros((c, c), F32)
    half = c // 2
    li = 0
    while half >= HG_SUB:
        blk = 2 * half
        refs = []
        for b0 in range(0, c, blk):
            r = b0 + half if reverse else b0 + half - 1
            refs.append(_bcast_row(cum, r, blk))
        ref = refs[0] if len(refs) == 1 else jnp.concatenate(refs, axis=0)
        qs = qh * jnp.exp2(cum - ref)
        ks = kh * jnp.exp2(ref - cum)
        a = a + jnp.where(masks[li], _dot_nt(qs.astype(BF16), ks.astype(BF16)), 0.0)
        half //= 2
        li += 1

    pieces = []
    for g in range(c // HG_SUB):
        r0 = g * HG_SUB
        qg = qh[r0:r0 + HG_SUB, :]
        cg = cum[r0:r0 + HG_SUB, :]
        for j in range(HG_SUB):
            pieces.append(qg * jnp.exp2(cg - _bcast_row(cum, r0 + j, HG_SUB)))
    rs = _dot_nt(jnp.concatenate(pieces, axis=0).astype(BF16), kh.astype(BF16))
    return o, st_new, a, rs


def _hg_finish(o, a, rs, vh, reverse):
    c = vh.shape[0]
    lane = lax.broadcasted_iota(jnp.int32, (HG_SUB, c), 1)
    rsub = lax.broadcasted_iota(jnp.int32, (HG_SUB, c), 0)
    diag = []
    for g in range(c // HG_SUB):
        acc = jnp.zeros((HG_SUB, c), F32)
        for j in range(HG_SUB):
            idx = g * HG_SUB + j
            causal = (rsub <= j) if reverse else (rsub >= j)
            acc = jnp.where((lane == idx) & causal, rs[idx * HG_SUB:(idx + 1) * HG_SUB, :], acc)
        diag.append(acc)
    a = a + jnp.concatenate(diag, axis=0)
    return o + _dot(a.astype(BF16), vh.astype(BF16))


def _hg_kernel(*refs, reverse, final, rows):
    if final:
        q_ref, f_ref, v_ref, la_ref, lb_ref, g_ref, op_ref, nw_ref, o_ref, s_ref = refs
    else:
        q_ref, f_ref, v_ref, la_ref, lb_ref, o_ref, s_ref = refs

    @pl.when(pl.program_id(1) == 0)
    def _():
        s_ref[...] = jnp.zeros_like(s_ref)

    c = HG_CHUNK
    ti = lax.broadcasted_iota(jnp.int32, (c, c), 0)
    si = lax.broadcasted_iota(jnp.int32, (c, c), 1)
    masks = []
    half = c // 2
    while half >= HG_SUB:
        blk = 2 * half
        same = (ti & -blk) == (si & -blk)
        if reverse:
            m = same & ((ti & half) == 0) & ((si & half) != 0)
        else:
            m = same & ((ti & half) != 0) & ((si & half) == 0)
        masks.append(m)
        half //= 2

    n_ch = rows // c
    order = range(n_ch - 1, -1, -1) if reverse else range(n_ch)
    gated = []
    for ci in order:
        r0 = ci * c
        z = f_ref[r0:r0 + c, :]
        z = z * LOG2E
        ls = jnp.minimum(z, 0.0) - jnp.log2(1.0 + jnp.exp2(-jnp.abs(z)))
        bt = lb_ref[...] + ls
        at = jnp.broadcast_to(la_ref[...], bt.shape)
        logf = jnp.maximum(at, bt) + jnp.log2(1.0 + jnp.exp2(-jnp.abs(at - bt)))
        gated.append((r0, _silu(q_ref[r0:r0 + c, :]), 1.0 - jnp.exp2(logf), v_ref[r0:r0 + c, :],
                      _cumsum_rows(logf, reverse)))

    states = [s_ref[hh] for hh in range(HG_HEADS)]
    heads = [slice(hh * HG_DK, (hh + 1) * HG_DK) for hh in range(HG_HEADS)]
    def finish(r0, vv, partial):
        outs = []
        for (o, a, rs), sl in zip(partial, heads):
            o = _hg_finish(o, a, rs, vv[:, sl], reverse)
            if final:
                tot = o + op_ref[r0:r0 + c, sl]
                ms = jnp.mean(tot * tot, axis=-1, keepdims=True)
                o = tot * lax.rsqrt(ms + EPS) * nw_ref[:, sl]
            outs.append(o)
        o_all = jnp.concatenate(outs, axis=1)
        if final:
            o_all = o_all * _silu(g_ref[r0:r0 + c, :])
        return r0, o_all.astype(o_ref.dtype)

    pending, done = None, []
    for r0, qf, kf, vv, cum in gated:
        partial = []
        for hh, sl in enumerate(heads):
            o, states[hh], a, rs = _hg_scores(qf[:, sl], kf[:, sl], vv[:, sl], cum[:, sl], states[hh],
                                              reverse, masks)
            partial.append((o, a, rs))
        if pending is not None:
            done.append(finish(*pending))
        pending = (r0, vv, partial)
    done.append(finish(*pending))
    for r0, o_all in done:
        o_ref[r0:r0 + c, :] = o_all
    for hh in range(HG_HEADS):
        s_ref[hh] = states[hh]


def _hg_direction(u, la, lb, batch, n_ctx, reverse, o_prev=None, nw=None):
    m = u.shape[0]
    t = m // batch
    nblk = t // TM
    ncb = n_ctx // TM
    final = o_prev is not None
    cb = OFF_HG // HG_WIDTH

    def blk(j):
        if not reverse:
            return j
        return jnp.where(j < ncb, ncb - 1 - j, nblk - 1 - (j - ncb))

    def tok(col):
        return pl.BlockSpec((TM, HG_WIDTH), lambda b, j: (b * nblk + blk(j), col))

    vec = pl.BlockSpec((1, HG_WIDTH), lambda b, j: (0, 0))
    f_col = cb + 2 if reverse else cb + 1
    in_specs = [tok(cb), tok(f_col), tok(cb + 3), vec, vec]
    args = [u, u, u, la.reshape(1, -1), lb.reshape(1, -1)]
    if final:
        in_specs += [tok(cb + 4), tok(0), vec]
        args += [u, o_prev, nw.reshape(1, -1)]
    return pl.pallas_call(
        functools.partial(_hg_kernel, reverse=reverse, final=final, rows=TM),
        grid=(batch, nblk),
        in_specs=in_specs,
        out_specs=tok(0),
        out_shape=jax.ShapeDtypeStruct((m, HG_WIDTH), BF16 if final else F32),
        scratch_shapes=[pltpu.VMEM((HG_HEADS, HG_DK, HG_DK), F32)],
        compiler_params=_params("parallel", "arbitrary"),
        name="hgrn2_bwd" if reverse else "hgrn2_fwd",
    )(*args)


def _swap_halves(x):
    lane = lax.broadcasted_iota(jnp.int32, x.shape, 1)
    return jnp.where((lane & 32) == 0, pltpu.roll(x, LANES - 32, 1), pltpu.roll(x, 32, 1))


def _head_norm_rope(x, w, cos, sin, gmat):
    x2 = x * x
    hi = x2.astype(BF16)
    lo = (x2 - hi.astype(F32)).astype(BF16)
    ms = (_dot(hi, gmat) + _dot(lo, gmat)) * (1.0 / HEAD_DIM)
    xn = x * lax.rsqrt(ms + EPS) * w
    return xn * cos + _swap_halves(xn) * sin


def _att_prep_kernel(q_ref, k_ref, v_ref, cos_ref, sin_ref, qw_ref, kw_ref, g_ref,
                     qo_ref, k0_ref, k1_ref, v0_ref, v1_ref):
    cos, sin, gmat = cos_ref[...], sin_ref[...], g_ref[...]
    qscale = (HEAD_DIM ** -0.5) * float(np.log2(np.e))
    for j in range(ATT_WIDTH // LANES):
        sl = slice(j * LANES, (j + 1) * LANES)
        qr = _head_norm_rope(q_ref[:, sl], qw_ref[...], cos, sin, gmat)
        qo_ref[:, sl] = (qr * qscale).astype(BF16)
    kr = _head_norm_rope(k_ref[...], kw_ref[...], cos, sin, gmat)
    lane = lax.broadcasted_iota(jnp.int32, kr.shape, 1)
    k0_ref[...] = jnp.where(lane < HEAD_DIM, kr, 0.0).astype(BF16)
    k1_ref[...] = jnp.where(lane >= HEAD_DIM, kr, 0.0).astype(BF16)
    vt = jnp.transpose(v_ref[...])
    tail = jnp.where(lax.broadcasted_iota(jnp.int32, (ATT_VT_ROWS - HEAD_DIM, vt.shape[1]), 0) == 0, 1.0, 0.0)
    v0_ref[...] = jnp.concatenate([vt[:HEAD_DIM], tail], axis=0).astype(BF16)
    v1_ref[...] = jnp.concatenate([vt[HEAD_DIM:], tail], axis=0).astype(BF16)


def _att_prep(u, cos, sin, qw, kw, gmat, batch):
    m = u.shape[0]
    tpb = m // batch // TM
    kvw = ATT_KV_HEADS * HEAD_DIM
    tab = pl.BlockSpec((TM, LANES), lambda i: (i % tpb, 0))
    vec = pl.BlockSpec((1, LANES), lambda i: (0, 0))
    kv_out = pl.BlockSpec((TM, kvw), lambda i: (i, 0))
    vt_out = pl.BlockSpec((ATT_VT_ROWS, TM), lambda i: (0, i))
    return pl.pallas_call(
        _att_prep_kernel,
        grid=(m // TM,),
        in_specs=[
            pl.BlockSpec((TM, ATT_WIDTH), lambda i: (i, OFF_ATT_Q // ATT_WIDTH)),
            pl.BlockSpec((TM, kvw), lambda i: (i, OFF_ATT_K // kvw)),
            pl.BlockSpec((TM, kvw), lambda i: (i, OFF_ATT_V // kvw)),
            tab, tab, vec, vec,
            pl.BlockSpec((LANES, LANES), lambda i: (0, 0)),
        ],
        out_specs=[pl.BlockSpec((TM, ATT_WIDTH), lambda i: (i, 0)), kv_out, kv_out, vt_out, vt_out],
        out_shape=[jax.ShapeDtypeStruct((m, ATT_WIDTH), BF16)] + [jax.ShapeDtypeStruct((m, kvw), BF16)] * 2
                  + [jax.ShapeDtypeStruct((ATT_VT_ROWS, m), BF16)] * 2,
        compiler_params=_params("parallel"),
        name="att_prep",
    )(u, u, u, cos, sin, qw, kw, gmat)


def _att_kernel(bound_ref, q_ref, k0_ref, k1_ref, v0_ref, v1_ref, o_ref, *, n_ctx, n_all, tq):
    bound = bound_ref[0, 0]

    def attend(r0, rows, n_keys, use_bound):
        q = q_ref[pl.ds(r0, rows), :]

        def head(k_ref, vt_ref):
            s = _dot_nt(k_ref[0:n_keys, :], q)
            shift = bound if use_bound else jnp.max(s, axis=0, keepdims=True)
            acc = _dot(vt_ref[:, 0:n_keys], jnp.exp2(s - shift).astype(BF16))
            return acc[:HEAD_DIM, :] / acc[HEAD_DIM:HEAD_DIM + 1, :]

        out = jnp.concatenate([head(k0_ref, v0_ref), head(k1_ref, v1_ref)], axis=0)
        o_ref[pl.ds(r0, rows), :] = jnp.transpose(out).astype(o_ref.dtype)

    def all_rows(use_bound):
        if n_ctx:
            attend(0, n_ctx, n_ctx, use_bound)

        def lat_chunk(i, _):
            attend(pl.multiple_of(n_ctx + i * tq, TM), tq, n_all, use_bound)
            return 0

        lax.fori_loop(0, (n_all - n_ctx) // tq, lat_chunk, 0)

    safe = bound <= ATT_SAFE_BOUND

    @pl.when(safe)
    def _():
        all_rows(True)

    @pl.when(jnp.logical_not(safe))
    def _():
        all_rows(False)


def _attention(bound, qr, k0, k1, v0, v1, batch, n_ctx):
    m = qr.shape[0]
    t = m // batch
    kvw = ATT_KV_HEADS * HEAD_DIM
    tq = ATT_TQ if (t - n_ctx) % ATT_TQ == 0 else TM
    qspec = pl.BlockSpec((t, LANES), lambda b, j: (b, j))
    kvspec = pl.BlockSpec((t, kvw), lambda b, j: (b, 0))
    vtspec = pl.BlockSpec((ATT_VT_ROWS, t), lambda b, j: (0, b))
    return pl.pallas_call(
        functools.partial(_att_kernel, n_ctx=n_ctx, n_all=t, tq=tq),
        grid=(batch, ATT_WIDTH // LANES),
        in_specs=[pl.BlockSpec(memory_space=pltpu.SMEM), qspec, kvspec, kvspec, vtspec, vtspec],
        out_specs=qspec,
        out_shape=jax.ShapeDtypeStruct((m, ATT_WIDTH), BF16),
        compiler_params=_params("parallel", "parallel"),
        name="attention",
    )(bound, qr, k0, k1, v0, v1)


def _scan_steps(a, u, reverse, axis):
    n = a.shape[axis]
    pos = lax.broadcasted_iota(jnp.int32, a.shape, axis)
    s = 1
    while s < n:
        if reverse:
            keep = pos < n - s
            ash, ush = pltpu.roll(a, n - s, axis), pltpu.roll(u, n - s, axis)
        else:
            keep = pos >= s
            ash, ush = pltpu.roll(a, s, axis), pltpu.roll(u, s, axis)
        u = u + a * jnp.where(keep, ush, 0.0)
        a = a * jnp.where(keep, ash, 1.0)
        s *= 2
    return a, u


def _scan_rows(a, u, carry, reverse, sa_ref, su_ref):
    n, width = a.shape
    groups = n // SUBLANES
    a3, u3 = _scan_steps(a.reshape(groups, SUBLANES, width), u.reshape(groups, SUBLANES, width), reverse, 1)
    a, u = a3.reshape(n, width), u3.reshape(n, width)
    sa_ref[...] = a
    su_ref[...] = u
    edge = 0 if reverse else SUBLANES - 1
    ae = sa_ref[pl.ds(edge, groups, stride=SUBLANES), :]
    ue = su_ref[pl.ds(edge, groups, stride=SUBLANES), :]
    ae, ue = _scan_steps(ae, ue, reverse, 0)
    hc = ue + ae * carry
    grow = lax.broadcasted_iota(jnp.int32, hc.shape, 0)
    if reverse:
        cin = jnp.where(grow < groups - 1, pltpu.roll(hc, groups - 1, 0), carry)
        out = hc[0:1, :]
    else:
        cin = jnp.where(grow >= 1, pltpu.roll(hc, 1, 0), carry)
        out = hc[groups - 1:groups, :]
    cin_rows = jnp.concatenate([_bcast_row(cin, i, SUBLANES) for i in range(groups)], axis=0)
    return u + a * cin_rows, out


def _lru_kernel(x_ref, g_ref, cw_ref, cb_ref, w_ref, b_ref, lam_ref, o_ref, xp_ref, xc_ref, h_ref,
                sa_ref, su_ref, *, n_ctx, n_tok):
    tt = TM
    pad = SUBLANES
    n_lat = n_tok - n_ctx
    lat0 = n_ctx + 2 * pad
    zeros = jnp.zeros((pad, LANES), F32)
    xp_ref[0:pad, :] = zeros
    xp_ref[pad + n_ctx:lat0, :] = zeros
    xp_ref[lat0 + n_lat:lat0 + n_lat + pad, :] = zeros
    xp_ref[pad:pad + n_ctx, :] = x_ref[0:n_ctx, :]
    xp_ref[lat0:lat0 + n_lat, :] = x_ref[n_ctx:n_tok, :]

    n_tiles = n_tok // tt
    ctx_tiles = n_ctx // tt

    def conv_tile(i, _):
        r0 = pl.multiple_of(i * tt, tt)
        p0 = pl.multiple_of(jnp.where(i < ctx_tiles, r0, r0 + pad), pad)
        ext = xp_ref[pl.ds(p0, tt + 2 * pad), :]
        acc = cb_ref[...] + cw_ref[0:1, :] * ext[pad - 2:pad - 2 + tt, :]
        for j in range(1, CONV_W):
            acc = acc + cw_ref[j:j + 1, :] * ext[pad - 2 + j:pad - 2 + j + tt, :]
        xc_ref[pl.ds(r0, tt), :] = acc
        return 0

    lax.fori_loop(0, n_tiles, conv_tile, 0)

    lam = lam_ref[...]
    sp = jnp.maximum(-lam, 0.0) + jnp.log1p(jnp.exp(-jnp.abs(lam)))
    rate = (-RG_C * LOG2E) * sp

    def coeffs(r0, d):
        xc = xc_ref[pl.ds(r0, tt), :]
        pre = _dot(xc.astype(BF16), w_ref[:, d * 2 * LANES:(d + 1) * 2 * LANES]) \
            + b_ref[:, d * 2 * LANES:(d + 1) * 2 * LANES]
        r = _sigmoid(pre[:, :LANES])
        ig = _sigmoid(pre[:, LANES:])
        a = jnp.exp2(r * rate[d:d + 1, :])
        t = 1.0 - a * a
        root = jnp.where(t > 0.0, t * lax.rsqrt(t), 0.0)
        return a, root * (ig * xc)

    def scan_tile(j, carry):
        cf, cb = carry
        rf = pl.multiple_of(j * tt, tt)
        ib = jnp.where(j < ctx_tiles, ctx_tiles - 1 - j, n_tiles - 1 - (j - ctx_tiles))
        rb = pl.multiple_of(ib * tt, tt)
        af, uf = coeffs(rf, 0)
        ab, ub = coeffs(rb, 1)
        hf, cf = _scan_rows(af, uf, cf, False, sa_ref.at[0], su_ref.at[0])
        hb, cb = _scan_rows(ab, ub, cb, True, sa_ref.at[1], su_ref.at[1])
        h_ref[0, pl.ds(rf, tt), :] = hf
        h_ref[1, pl.ds(rb, tt), :] = hb
        return cf, cb

    zero = jnp.zeros((1, LANES), F32)
    lax.fori_loop(0, n_tiles, scan_tile, (zero, zero))

    def gate_tile(i, _):
        r0 = pl.multiple_of(i * tt, tt)
        tot = h_ref[0, pl.ds(r0, tt), :] + h_ref[1, pl.ds(r0, tt), :]
        o_ref[pl.ds(r0, tt), :] = (tot * jax.nn.gelu(g_ref[pl.ds(r0, tt), :])).astype(o_ref.dtype)
        return 0

    lax.fori_loop(0, n_tiles, gate_tile, 0)


def _rglru(u, conv_w, conv_b, wcat, bcat, lam, batch, n_ctx):
    m = u.shape[0]
    t = m // batch
    ncg = LRU_WIDTH // LANES
    xcol = OFF_LRU // LANES
    return pl.pallas_call(
        functools.partial(_lru_kernel, n_ctx=n_ctx, n_tok=t),
        grid=(batch, ncg),
        in_specs=[
            pl.BlockSpec((t, LANES), lambda b, c: (b, xcol + c)),
            pl.BlockSpec((t, LANES), lambda b, c: (b, xcol + ncg + c)),
            pl.BlockSpec((CONV_W, LANES), lambda b, c: (0, c)),
            pl.BlockSpec((1, LANES), lambda b, c: (0, c)),
            pl.BlockSpec((None, LANES, 4 * LANES), lambda b, c: (c, 0, 0)),
            pl.BlockSpec((None, 1, 4 * LANES), lambda b, c: (c, 0, 0)),
            pl.BlockSpec((2, LANES), lambda b, c: (0, c)),
        ],
        out_specs=pl.BlockSpec((t, LANES), lambda b, c: (b, c)),
        out_shape=jax.ShapeDtypeStruct((m, LRU_WIDTH), BF16),
        scratch_shapes=[pltpu.VMEM((t + 3 * SUBLANES, LANES), F32),
                        pltpu.VMEM((t, LANES), F32),
                        pltpu.VMEM((2, t, LANES), F32),
                        pltpu.VMEM((2, TM, LANES), F32),
                        pltpu.VMEM((2, TM, LANES), F32)],
        compiler_params=_params("parallel", "parallel"),
        name="rglru",
    )(u, u, conv_w, conv_b.reshape(1, -1), wcat, bcat, lam)


def _merge_kernel(ya_ref, yb_ref, yc_ref, ml_ref, xs_ref, g1_ref, wa_ref, wb_ref, wc_ref, wo_ref, o_ref):
    d = D_MODEL
    acc = _sigmoid(ml_ref[:, 0:d]) * _dot(ya_ref[...], wa_ref[...])
    acc = acc + _sigmoid(ml_ref[:, d:2 * d]) * _dot(yb_ref[...], wb_ref[...])
    acc = acc + _sigmoid(ml_ref[:, 2 * d:3 * d]) * _dot(yc_ref[...], wc_ref[...])
    y = _dot(acc.astype(BF16), wo_ref[...])
    o_ref[...] = xs_ref[...] + g1_ref[...] * y


def _merge(ya, yb, yc, u, xs, mods, wa, wb, wc, wo, batch, n_ctx, drop_ctx):
    m, d = xs.shape
    tpb = m // batch // TM
    ctx_tiles = n_ctx // TM
    if drop_ctx:
        out_tpb = tpb - ctx_tiles
        src = lambda i: (i // out_tpb) * tpb + ctx_tiles + i % out_tpb
        row = lambda i: i // out_tpb
        m_out = batch * out_tpb * TM
    else:
        src = lambda i: i
        row = functools.partial(_mod_row, tiles_per_batch=tpb, ctx_tiles=ctx_tiles)
        m_out = m
    br = pl.BlockSpec((TM, HG_WIDTH), lambda i: (src(i), 0))
    full = lambda a: pl.BlockSpec(a.shape, lambda i: (0, 0))
    return pl.pallas_call(
        _merge_kernel,
        grid=(m_out // TM,),
        in_specs=[br, br, br,
                  pl.BlockSpec((TM, 3 * d), lambda i: (src(i), OFF_MERGE)),
                  pl.BlockSpec((TM, d), lambda i: (src(i), 0)),
                  pl.BlockSpec((None, 1, d), lambda i: (row(i), 0, 2)),
                  full(wa), full(wb), full(wc), full(wo)],
        out_specs=pl.BlockSpec((TM, d), lambda i: (i, 0)),
        out_shape=jax.ShapeDtypeStruct((m_out, d), F32),
        compiler_params=_params("parallel"),
        name="merge_out",
    )(ya, yb, yc, u, xs, mods, wa, wb, wc, wo)


def _swiglu_acc(hb, wg_ref, wu_ref, wd_ref):
    acc = jnp.zeros((hb.shape[0], wd_ref.shape[1]), F32)
    for c0 in range(0, D_FF, FF_CHUNK):
        a = _dot(hb, wg_ref[:, c0:c0 + FF_CHUNK])
        b = _dot(hb, wu_ref[:, c0:c0 + FF_CHUNK])
        acc = acc + _dot((_silu(a) * b).astype(BF16), wd_ref[c0:c0 + FF_CHUNK, :])
    return acc


def _final_norm(y, fw):
    ms = jnp.mean(y * y, axis=-1, keepdims=True)
    return y * lax.rsqrt(ms + EPS) * fw


def _ffn_kernel(x_ref, nw_ref, sc_ref, sh_ref, g2_ref, wg_ref, wu_ref, wd_ref, fw_ref, o_ref, *, final):
    x = x_ref[...]
    hb = _norm_mod(x, nw_ref[...], sc_ref[...], sh_ref[...]).astype(BF16)
    y = x + g2_ref[...] * _swiglu_acc(hb, wg_ref, wu_ref, wd_ref)
    o_ref[...] = _final_norm(y, fw_ref[...]) if final else y


def _mod_specs(row, d):
    return [pl.BlockSpec((None, 1, d), lambda i, *_: (row(i), 0, 4)),
            pl.BlockSpec((None, 1, d), lambda i, *_: (row(i), 0, 3)),
            pl.BlockSpec((None, 1, d), lambda i, *_: (row(i), 0, 5))]


def _ffn(xs, nw, mods, wg, wu, wd, fw, tiles_per_batch, ctx_tiles, final):
    m, d = xs.shape
    row = functools.partial(_mod_row, tiles_per_batch=tiles_per_batch, ctx_tiles=ctx_tiles)
    full = lambda a: pl.BlockSpec(a.shape, lambda i: (0, 0))
    vec = pl.BlockSpec((1, d), lambda i: (0, 0))
    return pl.pallas_call(
        functools.partial(_ffn_kernel, final=final),
        grid=(m // TM,),
        in_specs=[pl.BlockSpec((TM, d), lambda i: (i, 0)), vec] + _mod_specs(row, d)
                 + [full(wg), full(wu), full(wd), vec],
        out_specs=pl.BlockSpec((TM, d), lambda i: (i, 0)),
        out_shape=jax.ShapeDtypeStruct((m, d), F32),
        compiler_params=_params("parallel"),
        name="ffn_dense",
    )(xs, nw.reshape(1, d), mods, mods, mods, wg, wu, wd, fw.reshape(1, d))


def _top2(logits):
    lane = lax.broadcasted_iota(jnp.int32, logits.shape, 1)
    m1 = jnp.max(logits, axis=-1, keepdims=True)
    i1 = jnp.min(jnp.where(logits == m1, lane, LANES), axis=-1, keepdims=True)
    rest = jnp.where(lane == i1, -jnp.inf, logits)
    m2 = jnp.max(rest, axis=-1, keepdims=True)
    i2 = jnp.min(jnp.where(rest == m2, lane, LANES), axis=-1, keepdims=True)
    e2 = jnp.exp(m2 - m1)
    w1 = 1.0 / (1.0 + e2)
    return i1, i2, w1, e2 * w1


def _router_kernel(x_ref, nw_ref, sc_ref, sh_ref, rw_ref, meta_ref, wts_ref, cnt_ref, base_ref):
    @pl.when(pl.program_id(0) == 0)
    def _():
        base_ref[...] = jnp.zeros_like(base_ref)

    h = _norm_mod(x_ref[...], nw_ref[...], sc_ref[...], sh_ref[...])
    lane = lax.broadcasted_iota(jnp.int32, (h.shape[0], LANES), 1)
    logits = jnp.full((h.shape[0], LANES), -jnp.inf, F32)
    for e in range(N_EXPERTS):
        logits = jnp.where(lane == e, jnp.sum(h * rw_ref[e:e + 1, :], axis=-1, keepdims=True), logits)
    i1, i2, w1, w2 = _top2(logits)
    hit = jnp.where((lane == i1) | (lane == i2), 1.0, 0.0)
    tm = hit.shape[0]
    tri = lax.broadcasted_iota(jnp.int32, (tm, tm), 0) > lax.broadcasted_iota(jnp.int32, (tm, tm), 1)
    before = _dot(jnp.where(tri, 1.0, 0.0).astype(BF16), hit.astype(BF16)) + base_ref[...]
    r1 = jnp.sum(jnp.where(lane == i1, before, 0.0), axis=-1, keepdims=True).astype(jnp.int32)
    r2 = jnp.sum(jnp.where(lane == i2, before, 0.0), axis=-1, keepdims=True).astype(jnp.int32)
    base_ref[...] += jnp.sum(hit, axis=0, keepdims=True)
    cnt_ref[...] = base_ref[...]
    meta_ref[...] = jnp.where(lane == 0, i1, jnp.where(lane == 1, i2, jnp.where(lane == 2, r1,
                              jnp.where(lane == 3, r2, 0))))
    wts_ref[...] = jnp.where(lane == 0, w1, jnp.where(lane == 1, w2, 0.0))


def _router(xs, nw, mods, rw, row):
    m, d = xs.shape
    vec = pl.BlockSpec((1, d), lambda i: (0, 0))
    tok = pl.BlockSpec((TM, LANES), lambda i: (i, 0))
    return pl.pallas_call(
        _router_kernel,
        grid=(m // TM,),
        in_specs=[pl.BlockSpec((TM, d), lambda i: (i, 0)), vec] + _mod_specs(row, d)[:2]
                 + [pl.BlockSpec((N_EXPERTS, d), lambda i: (0, 0))],
        out_specs=[tok, tok, pl.BlockSpec((1, LANES), lambda i: (0, 0))],
        out_shape=[jax.ShapeDtypeStruct((m, LANES), jnp.int32), jax.ShapeDtypeStruct((m, LANES), F32),
                   jax.ShapeDtypeStruct((1, LANES), F32)],
        scratch_shapes=[pltpu.VMEM((1, LANES), F32)],
        compiler_params=_params("arbitrary"),
        name="moe_router",
    )(xs, nw.reshape(1, d), mods, mods, rw)


def _row_copy(src_ref, s, dst_ref, t, sem):
    return pltpu.make_async_copy(src_ref.at[pl.ds(s, 1)], dst_ref.at[pl.ds(t, 1)], sem)


def _dispatch_kernel(pos_ref, x_ref, nw_ref, sc_ref, sh_ref, xin_ref, xbuf_ref, h_ref, sem):
    del xin_ref
    h_ref[...] = _norm_mod(x_ref[...], nw_ref[...], sc_ref[...], sh_ref[...])
    tm = h_ref.shape[0]

    def issue(r, _):
        _row_copy(h_ref, r, xbuf_ref, pos_ref[0, 2 * r], sem).start(priority=0)
        _row_copy(h_ref, r, xbuf_ref, pos_ref[0, 2 * r + 1], sem).start(priority=1)
        return 0

    lax.fori_loop(0, tm, issue, 0, unroll=DMA_UNROLL)

    def drain(r, _):
        _row_copy(h_ref, 0, xbuf_ref, 0, sem).wait()
        _row_copy(h_ref, 0, xbuf_ref, 0, sem).wait()
        return 0

    lax.fori_loop(0, tm, drain, 0, unroll=DMA_UNROLL)


def _dispatch(xs, nw, mods, pos, row, n_rows):
    m, d = xs.shape
    vec = pl.BlockSpec((1, d), lambda i: (0, 0))
    slab = (d,)
    zeros = jnp.zeros((n_rows,) + slab, F32)
    return pl.pallas_call(
        _dispatch_kernel,
        grid=(m // TM,),
        in_specs=[pl.BlockSpec((None, 1, 2 * TM), lambda i: (i, 0, 0), memory_space=pltpu.SMEM),
                  pl.BlockSpec((TM, d), lambda i: (i, 0)), vec] + _mod_specs(row, d)[:2]
                 + [pl.BlockSpec(memory_space=pl.ANY)],
        out_specs=pl.BlockSpec(memory_space=pl.ANY),
        out_shape=jax.ShapeDtypeStruct((n_rows,) + slab, F32),
        scratch_shapes=[pltpu.VMEM((TM,) + slab, F32), pltpu.SemaphoreType.DMA(())],
        input_output_aliases={5: 0},
        compiler_params=_params("arbitrary"),
        name="moe_dispatch",
    )(pos, xs, nw.reshape(1, d), mods, mods, zeros)


def _expert_ffn_kernel(te_ref, nu_ref, x_ref, wg_ref, wu_ref, wd_ref, y_ref):
    del te_ref
    used = pl.program_id(0) < nu_ref[0]

    @pl.when(used)
    def _():
        y_ref[...] = _swiglu_acc(x_ref[...].astype(BF16), wg_ref, wu_ref, wd_ref)

    @pl.when(jnp.logical_not(used))
    def _():
        y_ref[...] = jnp.zeros_like(y_ref)


def _expert_ffn(xbuf, tile_expert, n_used, wg, wu, wd):
    n_rows = xbuf.shape[0]
    d = wg.shape[1]
    wspec = lambda shape: pl.BlockSpec((None,) + shape, lambda i, te, nu: (te[i], 0, 0))
    tok = pl.BlockSpec((TM, d), lambda i, te, nu: (i, 0))
    return pl.pallas_call(
        _expert_ffn_kernel,
        grid_spec=pltpu.PrefetchScalarGridSpec(
            num_scalar_prefetch=2,
            grid=(n_rows // TM,),
            in_specs=[tok, wspec((d, D_FF)), wspec((d, D_FF)), wspec((D_FF, d))],
            out_specs=tok,
        ),
        out_shape=jax.ShapeDtypeStruct(xbuf.shape, F32),
        compiler_params=_params("arbitrary"),
        name="moe_expert_ffn",
    )(tile_expert, n_used, xbuf, wg, wu, wd)


def _combine_kernel(pos_ref, x_ref, wts_ref, g2_ref, fw_ref, y_ref, o_ref, buf_ref, sem, *, final):
    tm = x_ref.shape[0]

    def issue(r, _):
        _row_copy(y_ref, pos_ref[0, 2 * r], buf_ref.at[0], r, sem).start(priority=0)
        _row_copy(y_ref, pos_ref[0, 2 * r + 1], buf_ref.at[1], r, sem).start(priority=1)
        return 0

    lax.fori_loop(0, tm, issue, 0, unroll=DMA_UNROLL)

    def drain(r, _):
        _row_copy(y_ref, 0, buf_ref.at[0], 0, sem).wait()
        _row_copy(y_ref, 0, buf_ref.at[0], 0, sem).wait()
        return 0

    lax.fori_loop(0, tm, drain, 0, unroll=DMA_UNROLL)
    w = wts_ref[...]
    f = w[:, 0:1] * buf_ref[0] + w[:, 1:2] * buf_ref[1]
    y = x_ref[...] + g2_ref[...] * f
    o_ref[...] = _final_norm(y, fw_ref[...]) if final else y


def _combine(xs, wts, mods, fw, ybuf, pos, row, final):
    m, d = xs.shape
    vec = pl.BlockSpec((1, d), lambda i: (0, 0))
    return pl.pallas_call(
        functools.partial(_combine_kernel, final=final),
        grid=(m // TM,),
        in_specs=[pl.BlockSpec((None, 1, 2 * TM), lambda i: (i, 0, 0), memory_space=pltpu.SMEM),
                  pl.BlockSpec((TM, d), lambda i: (i, 0)),
                  pl.BlockSpec((TM, LANES), lambda i: (i, 0)),
                  _mod_specs(row, d)[2], vec,
                  pl.BlockSpec(memory_space=pl.ANY)],
        out_specs=pl.BlockSpec((TM, d), lambda i: (i, 0)),
        out_shape=jax.ShapeDtypeStruct((m, d), F32),
        scratch_shapes=[pltpu.VMEM((2, TM) + ybuf.shape[1:], F32), pltpu.SemaphoreType.DMA(())],
        compiler_params=_params("arbitrary"),
        name="moe_combine",
    )(pos, xs, wts, mods, fw.reshape(1, d), ybuf)


def _moe(xs, nw, mods, rw, wg, wu, wd, fw, tiles_per_batch, ctx_tiles, final):
    m, d = xs.shape
    row = functools.partial(_mod_row, tiles_per_batch=tiles_per_batch, ctx_tiles=ctx_tiles)
    meta, wts, cnt = _router(xs, nw, mods, rw, row)
    counts = cnt[0, :N_EXPERTS].astype(jnp.int32)
    padded = (counts + TM - 1) // TM * TM
    ends = jnp.cumsum(padded)
    starts = ends - padded
    pos = jnp.stack([starts[meta[:, 0]] + meta[:, 2], starts[meta[:, 1]] + meta[:, 3]], axis=-1)
    pos = pos.reshape(m // TM, 1, 2 * TM)
    n_rows = 2 * m + N_EXPERTS * TM
    n_used = (ends[-1] // TM).astype(jnp.int32)
    tile_start = jnp.arange(n_rows // TM, dtype=jnp.int32) * TM
    tile_row = jnp.minimum(tile_start, ends[-1] - 1)
    tile_expert = jnp.sum((tile_row[:, None] >= ends[None, :]).astype(jnp.int32), axis=1)
    xbuf = _dispatch(xs, nw, mods, pos, row, n_rows)
    ybuf = _expert_ffn(xbuf, tile_expert, n_used.reshape(1), wg, wu, wd)
    return _combine(xs, wts, mods, fw, ybuf, pos, row, final)


def _cast_kernel(x_ref, o_ref):
    o_ref[...] = x_ref[...].astype(o_ref.dtype)


def _to_bf16(w, layer):
    cols = w.shape[-1]
    w2 = w.reshape(-1, cols)
    rows = TM * D_FF // cols
    blocks = w2.shape[0] // w.shape[0] // rows
    out = pl.pallas_call(
        _cast_kernel,
        grid=(blocks,),
        in_specs=[pl.BlockSpec((rows, cols), lambda i: (layer * blocks + i, 0))],
        out_specs=pl.BlockSpec((rows, cols), lambda i: (i, 0)),
        out_shape=jax.ShapeDtypeStruct((blocks * rows, cols), BF16),
        compiler_params=_params("parallel"),
        name="weights_to_bf16",
    )(w2)
    return out.reshape(w.shape[1:])


def _column_perm():
    sizes = (512, 512, 512, 512, 512, 512, 128, 128, 512, 512, 3 * D_MODEL)
    starts = np.concatenate([[0], np.cumsum(sizes)[:-1]])
    a_q, a_ff, a_fb, a_v, a_g, b_q, b_k, b_v, c_x, c_g, mrg = [
        np.arange(s, s + n) for s, n in zip(starts, sizes)]
    half = np.concatenate([np.arange(0, HEAD_DIM, 2), np.arange(1, HEAD_DIM, 2)])
    group = ATT_HEADS // ATT_KV_HEADS
    q_cols = []
    for j in range(group):
        q_cols += [b_q[j * HEAD_DIM + half], b_q[(group + j) * HEAD_DIM + half]]
    k_cols = [b_k[h * HEAD_DIM + half] for h in range(ATT_KV_HEADS)]
    return np.concatenate([mrg, a_q, a_ff, a_fb, a_v, a_g, c_x, c_g] + q_cols + k_cols + [b_v])


def _permute_columns(w, perm):
    cuts = np.flatnonzero(np.diff(perm) != 1) + 1
    parts, pending = [], []
    for run in np.split(perm, cuts):
        if len(run) >= LANES:
            if pending:
                parts.append(w[:, np.concatenate(pending)])
                pending = []
            parts.append(w[:, int(run[0]):int(run[-1]) + 1])
        else:
            pending.append(run)
    if pending:
        parts.append(w[:, np.concatenate(pending)])
    return jnp.concatenate(parts, axis=1)


def _att_out_rows():
    group = ATT_HEADS // ATT_KV_HEADS
    idx = []
    for j in range(group):
        idx += [np.arange(j * HEAD_DIM, (j + 1) * HEAD_DIM),
                np.arange((group + j) * HEAD_DIM, (group + j + 1) * HEAD_DIM)]
    return np.concatenate(idx)


def _rope_tables(n_ctx, n_lat):
    rows = n_lat // GRID_W
    row = jnp.repeat(jnp.arange(rows, dtype=F32), GRID_W)
    col = jnp.tile(jnp.arange(GRID_W, dtype=F32), rows)
    pairs = HEAD_DIM // 4
    freqs = ROPE_THETA ** (-jnp.arange(pairs, dtype=F32) / pairs)
    ang = jnp.concatenate([row[:, None] * freqs, col[:, None] * freqs], axis=-1)
    cos, sin = jnp.cos(ang), jnp.sin(ang)
    cos = jnp.concatenate([jnp.ones((n_ctx, HEAD_DIM // 2), F32), cos], axis=0)
    sin = jnp.concatenate([jnp.zeros((n_ctx, HEAD_DIM // 2), F32), sin], axis=0)
    reps = LANES // HEAD_DIM
    return (jnp.tile(jnp.concatenate([cos, cos], axis=-1), (1, reps)),
            jnp.tile(jnp.concatenate([-sin, sin], axis=-1), (1, reps)))


def _lru_weights(wa, wx, ba, bx):
    ncg = LRU_WIDTH // LANES

    def dense(w):
        w = w.reshape(ncg, 2, LRU_BLOCK_W, LRU_BLOCK_W)
        out = jnp.zeros((ncg, LANES, LANES), w.dtype)
        out = out.at[:, :LRU_BLOCK_W, :LRU_BLOCK_W].set(w[:, 0])
        return out.at[:, LRU_BLOCK_W:, LRU_BLOCK_W:].set(w[:, 1])

    wcat = jnp.concatenate([dense(wa[0]), dense(wx[0]), dense(wa[1]), dense(wx[1])], axis=-1).astype(BF16)
    bcat = jnp.concatenate([b.reshape(ncg, 1, LANES) for b in (ba[0], bx[0], ba[1], bx[1])], axis=-1)
    return wcat, bcat


def kernel(x, c, ctx, c_ctx, ada_w, ada_b, mix_norm_w, ffn_norm_w, w_in, hg_lb_logits, hg_norm_w, q_norm_w, k_norm_w, lru_conv_w, lru_conv_b, lru_wa, lru_ba, lru_wx, lru_bx, lru_lambda, w_br_a, w_br_b, w_br_c, w_out, ffn_w_gate, ffn_w_up, ffn_w_down, router_w, moe_w_gate, moe_w_up, moe_w_down, final_norm_w):
    batch, n_lat, d = x.shape
    n_ctx = ctx.shape[1]
    t = n_ctx + n_lat
    depth = ada_w.shape[0]
    assert n_ctx % TM == 0 and n_lat % TM == 0 and batch <= 4

    cvec = jnp.zeros((SUBLANES, d), F32).at[:batch].set(c).at[4].set(c_ctx)
    mods_all = _ada_mods(cvec, ada_w, ada_b).reshape(depth, SUBLANES, 1, 6 * d)

    p = jax.nn.softmax(hg_lb_logits.astype(F32), axis=0)
    cum = jnp.cumsum(p, axis=0)
    lbs = cum - cum[:1]
    log_lb, log_1m_lb = jnp.log(lbs) * LOG2E, jnp.log1p(-lbs) * LOG2E

    cos, sin = _rope_tables(n_ctx, n_lat)
    half = np.concatenate([np.arange(0, HEAD_DIM, 2), np.arange(1, HEAD_DIM, 2)])
    gidx = np.arange(LANES) // HEAD_DIM
    gmat = jnp.asarray(gidx[:, None] == gidx[None, :], BF16)
    col_perm = _column_perm()
    att_rows = _att_out_rows()

    xs = jnp.concatenate([ctx.astype(x.dtype), x], axis=1).reshape(batch * t, d)
    tpb, ctx_tiles = t // TM, n_ctx // TM
    for i in range(depth):
        last = i == depth - 1
        mods = mods_all[i]
        w_i = _permute_columns(w_in[i], col_perm).astype(BF16)
        u = _in_proj(xs, mix_norm_w[i], mods, w_i, tpb, ctx_tiles)

        o_f = _hg_direction(u, log_lb[i, 0], log_1m_lb[i, 0], batch, n_ctx, False)
        y_a = _hg_direction(u, log_lb[i, 1], log_1m_lb[i, 1], batch, n_ctx, True, o_f, hg_norm_w[i])

        reps = LANES // HEAD_DIM
        qw = jnp.tile(q_norm_w[i][half], reps).reshape(1, LANES)
        kw = jnp.tile(k_norm_w[i][half], reps).reshape(1, LANES)
        qr, k0, k1, v0, v1 = _att_prep(u, cos, sin, qw, kw, gmat, batch)
        bound = (1.02 * HEAD_DIM ** 0.5 * float(np.log2(np.e))
                 * jnp.max(jnp.abs(q_norm_w[i])) * jnp.max(jnp.abs(k_norm_w[i]))).reshape(1, 1)
        y_b = _attention(bound, qr, k0, k1, v0, v1, batch, n_ctx)

        wcat, bcat = _lru_weights(lru_wa[i], lru_wx[i], lru_ba[i], lru_bx[i])
        y_c = _rglru(u, lru_conv_w[i], lru_conv_b[i], wcat, bcat, lru_lambda[i], batch, n_ctx)

        xs = _merge(y_a, y_b, y_c, u, xs, mods, w_br_a[i].astype(BF16), w_br_b[i][att_rows].astype(BF16),
                    w_br_c[i].astype(BF16), w_out[i].astype(BF16), batch, n_ctx, last)
        if last:
            tpb, ctx_tiles = n_lat // TM, 0
        if i % 2 == 0:
            j = i // 2
            xs = _ffn(xs, ffn_norm_w[i], mods, _to_bf16(ffn_w_gate, j), _to_bf16(ffn_w_up, j),
                      _to_bf16(ffn_w_down, j), final_norm_w, tpb, ctx_tiles, last)
        else:
            j = i // 2
            rw = router_w[j].T
            xs = _moe(xs, ffn_norm_w[i], mods, rw, _to_bf16(moe_w_gate, j), _to_bf16(moe_w_up, j),
                      _to_bf16(moe_w_down, j), final_norm_w, tpb, ctx_tiles, last)
    return xs.reshape(batch, n_lat, d)
```

```python
import functools

import numpy as np
import jax
import jax.numpy as jnp
from jax import lax
from jax.experimental import pallas as pl
from jax.experimental.pallas import tpu as pltpu

F32 = jnp.float32
BF16 = jnp.bfloat16

D_MODEL = 1024
DEPTH = 4
GRID_W = 64
HG_HEADS = 4
HG_DK = 128
HG_WIDTH = 512
ATT_HEADS = 8
ATT_KV_HEADS = 2
HEAD_DIM = 64
ATT_WIDTH = 512
ROPE_THETA = 10000.0
LRU_WIDTH = 512
LRU_BLOCK_W = 64
CONV_W = 4
RG_C = 8.0
D_FF = 2816
N_EXPERTS = 8
EPS = 1e-6

LOG2E = float(np.log2(np.e))
LANES = 128
SUBLANES = 8
TM = 256
HG_CHUNK = 64
HG_SUB = 8
ATT_TQ = 512
ATT_SAFE_BOUND = 60.0
ATT_VT_ROWS = 80
FF_CHUNK = 2816
IN_PROJ_CHUNK = 512
DMA_UNROLL = 8
VMEM_LIMIT = 56 * 1024 * 1024

OFF_MERGE = 0
OFF_HG = 3 * D_MODEL
OFF_LRU = OFF_HG + 5 * HG_WIDTH
OFF_ATT_Q = OFF_LRU + 2 * LRU_WIDTH
OFF_ATT_K = OFF_ATT_Q + ATT_WIDTH
OFF_ATT_V = OFF_ATT_K + ATT_KV_HEADS * HEAD_DIM
IN_DIM = OFF_ATT_V + ATT_KV_HEADS * HEAD_DIM


def _dot(a, b):
    return jnp.dot(a, b, preferred_element_type=F32)


def _dot_nt(a, b):
    return lax.dot_general(a, b, (((1,), (1,)), ((), ())), preferred_element_type=F32)


def _dot_tn(a, b):
    return lax.dot_general(a, b, (((0,), (0,)), ((), ())), preferred_element_type=F32)


def _sigmoid(x):
    return 1.0 / (1.0 + jnp.exp(-x))


def _silu(x):
    return x * _sigmoid(x)


def _params(*sem):
    return pltpu.CompilerParams(dimension_semantics=sem, vmem_limit_bytes=VMEM_LIMIT)


def _mod_row(i, tiles_per_batch, ctx_tiles):
    if ctx_tiles == 0:
        return i // tiles_per_batch
    return jnp.where(i % tiles_per_batch < ctx_tiles, 4, i // tiles_per_batch)


def _norm_mod(x, nw, sc, sh):
    ms = jnp.mean(x * x, axis=-1, keepdims=True)
    return (x * lax.rsqrt(ms + EPS) * nw) * (1.0 + sc) + sh


def _ada_kernel(c_ref, w_ref, b_ref, o_ref):
    s = _silu(c_ref[...])
    o_ref[...] = _dot(s.astype(BF16), w_ref[...].astype(BF16)) + b_ref[...]


def _ada_mods(cvec, ada_w, ada_b):
    depth, d, n = ada_w.shape
    tn = 1536
    return pl.pallas_call(
        _ada_kernel,
        grid=(depth, n // tn),
        in_specs=[
            pl.BlockSpec((SUBLANES, d), lambda l, j: (0, 0)),
            pl.BlockSpec((None, d, tn), lambda l, j: (l, 0, j)),
            pl.BlockSpec((None, 1, tn), lambda l, j: (l, 0, j)),
        ],
        out_specs=pl.BlockSpec((None, SUBLANES, tn), lambda l, j: (l, 0, j)),
        out_shape=jax.ShapeDtypeStruct((depth, SUBLANES, n), F32),
        compiler_params=_params("parallel", "parallel"),
        name="ada_mods",
    )(cvec, ada_w, ada_b.reshape(depth, 1, n))


def _in_proj_kernel(x_ref, nw_ref, sc_ref, sh_ref, w_ref, o_ref):
    hb = _norm_mod(x_ref[...], nw_ref[...], sc_ref[...], sh_ref[...]).astype(BF16)
    n = w_ref.shape[1]
    step = IN_PROJ_CHUNK
    for c0 in range(0, n, step):
        cw = min(step, n - c0)
        o_ref[:, c0:c0 + cw] = _dot(hb, w_ref[:, c0:c0 + cw]).astype(o_ref.dtype)


def _in_proj(xs, nw, mods, w, tiles_per_batch, ctx_tiles):
    m, d = xs.shape
    n = w.shape[1]
    row = functools.partial(_mod_row, tiles_per_batch=tiles_per_batch, ctx_tiles=ctx_tiles)
    return pl.pallas_call(
        _in_proj_kernel,
        grid=(m // TM,),
        in_specs=[
            pl.BlockSpec((TM, d), lambda i: (i, 0)),
            pl.BlockSpec((1, d), lambda i: (0, 0)),
            pl.BlockSpec((None, 1, d), lambda i: (row(i), 0, 1)),
            pl.BlockSpec((None, 1, d), lambda i: (row(i), 0, 0)),
            pl.BlockSpec((d, n), lambda i: (0, 0)),
        ],
        out_specs=pl.BlockSpec((TM, n), lambda i: (i, 0)),
        out_shape=jax.ShapeDtypeStruct((m, n), F32),
        compiler_params=_params("parallel"),
        name="in_proj",
    )(xs, nw.reshape(1, d), mods, mods, w)


def _cumsum_rows(x, reverse):
    n = x.shape[0]
    r = lax.broadcasted_iota(jnp.int32, (n, n), 0)
    c = lax.broadcasted_iota(jnp.int32, (n, n), 1)
    tri = jnp.where((c >= r) if reverse else (c <= r), 1.0, 0.0).astype(BF16)
    p1 = x.astype(BF16)
    rest = x - p1.astype(F32)
    p2 = rest.astype(BF16)
    p3 = (rest - p2.astype(F32)).astype(BF16)
    return _dot(tri, p1) + _dot(tri, p2) + _dot(tri, p3)


def _bcast_row(x, r, rows):
    return jnp.broadcast_to(x[r:r + 1, :], (rows, x.shape[1]))


def _hg_scores(qh, kh, vh, cum, st, reverse, masks):
    c = qh.shape[0]
    last = cum[0:1, :] if reverse else cum[c - 1:c, :]
    o = _dot_nt((qh * jnp.exp2(cum)).astype(BF16), st.astype(BF16))
    kd = kh * jnp.exp2(last - cum)
    st_new = st * jnp.exp2(last) + _dot_tn(vh.astype(BF16), kd.astype(BF16))

    a = jnp.zeros((c, c), F32)
    half = c // 2
    li = 0
    while half >= HG_SUB:
        blk = 2 * half
        refs = []
        for b0 in range(0, c, blk):
            r = b0 + half if reverse else b0 + half - 1
            refs.append(_bcast_row(cum, r, blk))
        ref = refs[0] if len(refs) == 1 else jnp.concatenate(refs, axis=0)
        qs = qh * jnp.exp2(cum - ref)
        ks = kh * jnp.exp2(ref - cum)
        a = a + jnp.where(masks[li], _dot_nt(qs.astype(BF16), ks.astype(BF16)), 0.0)
        half //= 2
        li += 1

    pieces = []
    for g in range(c // HG_SUB):
        r0 = g * HG_SUB
        qg = qh[r0:r0 + HG_SUB, :]
        cg = cum[r0:r0 + HG_SUB, :]
        for j in range(HG_SUB):
            pieces.append(qg * jnp.exp2(cg - _bcast_row(cum, r0 + j, HG_SUB)))
    rs = _dot_nt(jnp.concatenate(pieces, axis=0).astype(BF16), kh.astype(BF16))
    return o, st_new, a, rs


def _hg_finish(o, a, rs, vh, reverse):
    c = vh.shape[0]
    lane = lax.broadcasted_iota(jnp.int32, (HG_SUB, c), 1)
    rsub = lax.broadcasted_iota(jnp.int32, (HG_SUB, c), 0)
    diag = []
    for g in range(c // HG_SUB):
        acc = jnp.zeros((HG_SUB, c), F32)
        for j in range(HG_SUB):
            idx = g * HG_SUB + j
            causal = (rsub <= j) if reverse else (rsub >= j)
            acc = jnp.where((lane == idx) & causal, rs[idx * HG_SUB:(idx + 1) * HG_SUB, :], acc)
        diag.append(acc)
    a = a + jnp.concatenate(diag, axis=0)
    return o + _dot(a.astype(BF16), vh.astype(BF16))


def _hg_kernel(*refs, reverse, final, rows):
    if final:
        q_ref, f_ref, v_ref, la_ref, lb_ref, g_ref, op_ref, nw_ref, o_ref, s_ref = refs
    else:
        q_ref, f_ref, v_ref, la_ref, lb_ref, o_ref, s_ref = refs

    @pl.when(pl.program_id(1) == 0)
    def _():
        s_ref[...] = jnp.zeros_like(s_ref)

    c = HG_CHUNK
    ti = lax.broadcasted_iota(jnp.int32, (c, c), 0)
    si = lax.broadcasted_iota(jnp.int32, (c, c), 1)
    masks = []
    half = c // 2
    while half >= HG_SUB:
        blk = 2 * half
        same = (ti & -blk) == (si & -blk)
        if reverse:
            m = same & ((ti & half) == 0) & ((si & half) != 0)
        else:
            m = same & ((ti & half) != 0) & ((si & half) == 0)
        masks.append(m)
        half //= 2

    n_ch = rows // c
    order = range(n_ch - 1, -1, -1) if reverse else range(n_ch)
    gated = []
    for ci in order:
        r0 = ci * c
        z = f_ref[r0:r0 + c, :]
        z = z * LOG2E
        ls = jnp.minimum(z, 0.0) - jnp.log2(1.0 + jnp.exp2(-jnp.abs(z)))
        bt = lb_ref[...] + ls
        at = jnp.broadcast_to(la_ref[...], bt.shape)
        logf = jnp.maximum(at, bt) + jnp.log2(1.0 + jnp.exp2(-jnp.abs(at - bt)))
        gated.append((r0, _silu(q_ref[r0:r0 + c, :]), 1.0 - jnp.exp2(logf), v_ref[r0:r0 + c, :],
                      _cumsum_rows(logf, reverse)))

    states = [s_ref[hh] for hh in range(HG_HEADS)]
    heads = [slice(hh * HG_DK, (hh + 1) * HG_DK) for hh in range(HG_HEADS)]
    def finish(r0, vv, partial):
        outs = []
        for (o, a, rs), sl in zip(partial, heads):
            o = _hg_finish(o, a, rs, vv[:, sl], reverse)
            if final:
                tot = o + op_ref[r0:r0 + c, sl]
                ms = jnp.mean(tot * tot, axis=-1, keepdims=True)
                o = tot * lax.rsqrt(ms + EPS) * nw_ref[:, sl]
            outs.append(o)
        o_all = jnp.concatenate(outs, axis=1)
        if final:
            o_all = o_all * _silu(g_ref[r0:r0 + c, :])
        return r0, o_all.astype(o_ref.dtype)

    pending, done = None, []
    for r0, qf, kf, vv, cum in gated:
        partial = []
        for hh, sl in enumerate(heads):
            o, states[hh], a, rs = _hg_scores(qf[:, sl], kf[:, sl], vv[:, sl], cum[:, sl], states[hh],
                                              reverse, masks)
            partial.append((o, a, rs))
        if pending is not None:
            done.append(finish(*pending))
        pending = (r0, vv, partial)
    done.append(finish(*pending))
    for r0, o_all in done:
        o_ref[r0:r0 + c, :] = o_all
    for hh in range(HG_HEADS):
        s_ref[hh] = states[hh]


def _hg_direction(u, la, lb, batch, n_ctx, reverse, o_prev=None, nw=None):
    m = u.shape[0]
    t = m // batch
    nblk = t // TM
    ncb = n_ctx // TM
    final = o_prev is not None
    cb = OFF_HG // HG_WIDTH

    def blk(j):
        if not reverse:
            return j
        return jnp.where(j < ncb, ncb - 1 - j, nblk - 1 - (j - ncb))

    def tok(col):
        return pl.BlockSpec((TM, HG_WIDTH), lambda b, j: (b * nblk + blk(j), col))

    vec = pl.BlockSpec((1, HG_WIDTH), lambda b, j: (0, 0))
    f_col = cb + 2 if reverse else cb + 1
    in_specs = [tok(cb), tok(f_col), tok(cb + 3), vec, vec]
    args = [u, u, u, la.reshape(1, -1), lb.reshape(1, -1)]
    if final:
        in_specs += [tok(cb + 4), tok(0), vec]
        args += [u, o_prev, nw.reshape(1, -1)]
    return pl.pallas_call(
        functools.partial(_hg_kernel, reverse=reverse, final=final, rows=TM),
        grid=(batch, nblk),
        in_specs=in_specs,
        out_specs=tok(0),
        out_shape=jax.ShapeDtypeStruct((m, HG_WIDTH), BF16 if final else F32),
        scratch_shapes=[pltpu.VMEM((HG_HEADS, HG_DK, HG_DK), F32)],
        compiler_params=_params("parallel", "arbitrary"),
        name="hgrn2_bwd" if reverse else "hgrn2_fwd",
    )(*args)


def _swap_halves(x):
    lane = lax.broadcasted_iota(jnp.int32, x.shape, 1)
    return jnp.where((lane & 32) == 0, pltpu.roll(x, LANES - 32, 1), pltpu.roll(x, 32, 1))


def _head_norm_rope(x, w, cos, sin, gmat):
    x2 = x * x
    hi = x2.astype(BF16)
    lo = (x2 - hi.astype(F32)).astype(BF16)
    ms = (_dot(hi, gmat) + _dot(lo, gmat)) * (1.0 / HEAD_DIM)
    xn = x * lax.rsqrt(ms + EPS) * w
    return xn * cos + _swap_halves(xn) * sin


def _att_prep_kernel(q_ref, k_ref, v_ref, cos_ref, sin_ref, qw_ref, kw_ref, g_ref,
                     qo_ref, k0_ref, k1_ref, v0_ref, v1_ref):
    cos, sin, gmat = cos_ref[...], sin_ref[...], g_ref[...]
    qscale = (HEAD_DIM ** -0.5) * float(np.log2(np.e))
    for j in range(ATT_WIDTH // LANES):
        sl = slice(j * LANES, (j + 1) * LANES)
        qr = _head_norm_rope(q_ref[:, sl], qw_ref[...], cos, sin, gmat)
        qo_ref[:, sl] = (qr * qscale).astype(BF16)
    kr = _head_norm_rope(k_ref[...], kw_ref[...], cos, sin, gmat)
    lane = lax.broadcasted_iota(jnp.int32, kr.shape, 1)
    k0_ref[...] = jnp.where(lane < HEAD_DIM, kr, 0.0).astype(BF16)
    k1_ref[...] = jnp.where(lane >= HEAD_DIM, kr, 0.0).astype(BF16)
    vt = jnp.transpose(v_ref[...])
    tail = jnp.where(lax.broadcasted_iota(jnp.int32, (ATT_VT_ROWS - HEAD_DIM, vt.shape[1]), 0) == 0, 1.0, 0.0)
    v0_ref[...] = jnp.concatenate([vt[:HEAD_DIM], tail], axis=0).astype(BF16)
    v1_ref[...] = jnp.concatenate([vt[HEAD_DIM:], tail], axis=0).astype(BF16)


def _att_prep(u, cos, sin, qw, kw, gmat, batch):
    m = u.shape[0]
    tpb = m // batch // TM
    kvw = ATT_KV_HEADS * HEAD_DIM
    tab = pl.BlockSpec((TM, LANES), lambda i: (i % tpb, 0))
    vec = pl.BlockSpec((1, LANES), lambda i: (0, 0))
    kv_out = pl.BlockSpec((TM, kvw), lambda i: (i, 0))
    vt_out = pl.BlockSpec((ATT_VT_ROWS, TM), lambda i: (0, i))
    return pl.pallas_call(
        _att_prep_kernel,
        grid=(m // TM,),
        in_specs=[
            pl.BlockSpec((TM, ATT_WIDTH), lambda i: (i, OFF_ATT_Q // ATT_WIDTH)),
            pl.BlockSpec((TM, kvw), lambda i: (i, OFF_ATT_K // kvw)),
            pl.BlockSpec((TM, kvw), lambda i: (i, OFF_ATT_V // kvw)),
            tab, tab, vec, vec,
            pl.BlockSpec((LANES, LANES), lambda i: (0, 0)),
        ],
        out_specs=[pl.BlockSpec((TM, ATT_WIDTH), lambda i: (i, 0)), kv_out, kv_out, vt_out, vt_out],
        out_shape=[jax.ShapeDtypeStruct((m, ATT_WIDTH), BF16)] + [jax.ShapeDtypeStruct((m, kvw), BF16)] * 2
                  + [jax.ShapeDtypeStruct((ATT_VT_ROWS, m), BF16)] * 2,
        compiler_params=_params("parallel"),
        name="att_prep",
    )(u, u, u, cos, sin, qw, kw, gmat)


def _att_kernel(bound_ref, q_ref, k0_ref, k1_ref, v0_ref, v1_ref, o_ref, *, n_ctx, n_all, tq):
    bound = bound_ref[0, 0]

    def attend(r0, rows, n_keys, use_bound):
        q = q_ref[pl.ds(r0, rows), :]
        outs = []
        for k_ref, vt_ref in ((k0_ref, v0_ref), (k1_ref, v1_ref)):
            s = _dot_nt(k_ref[0:n_keys, :], q)
            shift = bound if use_bound else jnp.max(s, axis=0, keepdims=True)
            acc = _dot(vt_ref[:, 0:n_keys], jnp.exp2(s - shift).astype(BF16))
            outs.append(acc[:HEAD_DIM, :] / acc[HEAD_DIM:HEAD_DIM + 1, :])
        out = jnp.concatenate(outs, axis=0)
        o_ref[pl.ds(r0, rows), :] = jnp.transpose(out).astype(o_ref.dtype)

    def all_rows(use_bound):
        if n_ctx:
            attend(0, n_ctx, n_ctx, use_bound)

        def lat_chunk(i, _):
            attend(pl.multiple_of(n_ctx + i * tq, TM), tq, n_all, use_bound)
            return 0

        lax.fori_loop(0, (n_all - n_ctx) // tq, lat_chunk, 0)

    safe = bound <= ATT_SAFE_BOUND

    @pl.when(safe)
    def _():
        all_rows(True)

    @pl.when(jnp.logical_not(safe))
    def _():
        all_rows(False)


def _attention(bound, qr, k0, k1, v0, v1, batch, n_ctx):
    m = qr.shape[0]
    t = m // batch
    kvw = ATT_KV_HEADS * HEAD_DIM
    tq = ATT_TQ if (t - n_ctx) % ATT_TQ == 0 else TM
    qspec = pl.BlockSpec((t, LANES), lambda b, j: (b, j))
    kvspec = pl.BlockSpec((t, kvw), lambda b, j: (b, 0))
    vtspec = pl.BlockSpec((ATT_VT_ROWS, t), lambda b, j: (0, b))
    return pl.pallas_call(
        functools.partial(_att_kernel, n_ctx=n_ctx, n_all=t, tq=tq),
        grid=(batch, ATT_WIDTH // LANES),
        in_specs=[pl.BlockSpec(memory_space=pltpu.SMEM), qspec, kvspec, kvspec, vtspec, vtspec],
        out_specs=qspec,
        out_shape=jax.ShapeDtypeStruct((m, ATT_WIDTH), BF16),
        compiler_params=_params("parallel", "parallel"),
        name="attention",
    )(bound, qr, k0, k1, v0, v1)


def _scan_steps(a, u, reverse, axis):
    n = a.shape[axis]
    pos = lax.broadcasted_iota(jnp.int32, a.shape, axis)
    s = 1
    while s < n:
        if reverse:
            keep = pos < n - s
            ash, ush = pltpu.roll(a, n - s, axis), pltpu.roll(u, n - s, axis)
        else:
            keep = pos >= s
            ash, ush = pltpu.roll(a, s, axis), pltpu.roll(u, s, axis)
        u = u + a * jnp.where(keep, ush, 0.0)
        a = a * jnp.where(keep, ash, 1.0)
        s *= 2
    return a, u


def _scan_rows(a, u, carry, reverse, sa_ref, su_ref):
    n, width = a.shape
    groups = n // SUBLANES
    a3, u3 = _scan_steps(a.reshape(groups, SUBLANES, width), u.reshape(groups, SUBLANES, width), reverse, 1)
    a, u = a3.reshape(n, width), u3.reshape(n, width)
    sa_ref[...] = a
    su_ref[...] = u
    edge = 0 if reverse else SUBLANES - 1
    ae = sa_ref[pl.ds(edge, groups, stride=SUBLANES), :]
    ue = su_ref[pl.ds(edge, groups, stride=SUBLANES), :]
    ae, ue = _scan_steps(ae, ue, reverse, 0)
    hc = ue + ae * carry
    grow = lax.broadcasted_iota(jnp.int32, hc.shape, 0)
    if reverse:
        cin = jnp.where(grow < groups - 1, pltpu.roll(hc, groups - 1, 0), carry)
        out = hc[0:1, :]
    else:
        cin = jnp.where(grow >= 1, pltpu.roll(hc, 1, 0), carry)
        out = hc[groups - 1:groups, :]
    cin_rows = jnp.concatenate([_bcast_row(cin, i, SUBLANES) for i in range(groups)], axis=0)
    return u + a * cin_rows, out


def _lru_kernel(x_ref, g_ref, cw_ref, cb_ref, w_ref, b_ref, lam_ref, o_ref, xp_ref, xc_ref, h_ref,
                sa_ref, su_ref, *, n_ctx, n_tok):
    tt = TM
    pad = SUBLANES
    n_lat = n_tok - n_ctx
    lat0 = n_ctx + 2 * pad
    zeros = jnp.zeros((pad, LANES), F32)
    xp_ref[0:pad, :] = zeros
    xp_ref[pad + n_ctx:lat0, :] = zeros
    xp_ref[lat0 + n_lat:lat0 + n_lat + pad, :] = zeros
    xp_ref[pad:pad + n_ctx, :] = x_ref[0:n_ctx, :]
    xp_ref[lat0:lat0 + n_lat, :] = x_ref[n_ctx:n_tok, :]

    n_tiles = n_tok // tt
    ctx_tiles = n_ctx // tt

    def conv_tile(i, _):
        r0 = pl.multiple_of(i * tt, tt)
        p0 = pl.multiple_of(jnp.where(i < ctx_tiles, r0, r0 + pad), pad)
        ext = xp_ref[pl.ds(p0, tt + 2 * pad), :]
        acc = cb_ref[...] + cw_ref[0:1, :] * ext[pad - 2:pad - 2 + tt, :]
        for j in range(1, CONV_W):
            acc = acc + cw_ref[j:j + 1, :] * ext[pad - 2 + j:pad - 2 + j + tt, :]
        xc_ref[pl.ds(r0, tt), :] = acc
        return 0

    lax.fori_loop(0, n_tiles, conv_tile, 0)

    lam = lam_ref[...]
    sp = jnp.maximum(-lam, 0.0) + jnp.log1p(jnp.exp(-jnp.abs(lam)))
    rate = (-RG_C * LOG2E) * sp

    def coeffs(r0, d):
        xc = xc_ref[pl.ds(r0, tt), :]
        pre = _dot(xc.astype(BF16), w_ref[:, d * 2 * LANES:(d + 1) * 2 * LANES]) \
            + b_ref[:, d * 2 * LANES:(d + 1) * 2 * LANES]
        r = _sigmoid(pre[:, :LANES])
        ig = _sigmoid(pre[:, LANES:])
        a = jnp.exp2(r * rate[d:d + 1, :])
        t = 1.0 - a * a
        root = jnp.where(t > 0.0, t * lax.rsqrt(t), 0.0)
        return a, root * (ig * xc)

    def scan_tile(j, carry):
        cf, cb = carry
        rf = pl.multiple_of(j * tt, tt)
        ib = jnp.where(j < ctx_tiles, ctx_tiles - 1 - j, n_tiles - 1 - (j - ctx_tiles))
        rb = pl.multiple_of(ib * tt, tt)
        af, uf = coeffs(rf, 0)
        ab, ub = coeffs(rb, 1)
        hf, cf = _scan_rows(af, uf, cf, False, sa_ref.at[0], su_ref.at[0])
        hb, cb = _scan_rows(ab, ub, cb, True, sa_ref.at[1], su_ref.at[1])
        h_ref[0, pl.ds(rf, tt), :] = hf
        h_ref[1, pl.ds(rb, tt), :] = hb
        return cf, cb

    zero = jnp.zeros((1, LANES), F32)
    lax.fori_loop(0, n_tiles, scan_tile, (zero, zero))

    def gate_tile(i, _):
        r0 = pl.multiple_of(i * tt, tt)
        tot = h_ref[0, pl.ds(r0, tt), :] + h_ref[1, pl.ds(r0, tt), :]
        o_ref[pl.ds(r0, tt), :] = (tot * jax.nn.gelu(g_ref[pl.ds(r0, tt), :])).astype(o_ref.dtype)
        return 0

    lax.fori_loop(0, n_tiles, gate_tile, 0)


def _rglru(u, conv_w, conv_b, wcat, bcat, lam, batch, n_ctx):
    m = u.shape[0]
    t = m // batch
    ncg = LRU_WIDTH // LANES
    xcol = OFF_LRU // LANES
    return pl.pallas_call(
        functools.partial(_lru_kernel, n_ctx=n_ctx, n_tok=t),
        grid=(batch, ncg),
        in_specs=[
            pl.BlockSpec((t, LANES), lambda b, c: (b, xcol + c)),
            pl.BlockSpec((t, LANES), lambda b, c: (b, xcol + ncg + c)),
            pl.BlockSpec((CONV_W, LANES), lambda b, c: (0, c)),
            pl.BlockSpec((1, LANES), lambda b, c: (0, c)),
            pl.BlockSpec((None, LANES, 4 * LANES), lambda b, c: (c, 0, 0)),
            pl.BlockSpec((None, 1, 4 * LANES), lambda b, c: (c, 0, 0)),
            pl.BlockSpec((2, LANES), lambda b, c: (0, c)),
        ],
        out_specs=pl.BlockSpec((t, LANES), lambda b, c: (b, c)),
        out_shape=jax.ShapeDtypeStruct((m, LRU_WIDTH), BF16),
        scratch_shapes=[pltpu.VMEM((t + 3 * SUBLANES, LANES), F32),
                        pltpu.VMEM((t, LANES), F32),
                        pltpu.VMEM((2, t, LANES), F32),
                        pltpu.VMEM((2, TM, LANES), F32),
                        pltpu.VMEM((2, TM, LANES), F32)],
        compiler_params=_params("parallel", "parallel"),
        name="rglru",
    )(u, u, conv_w, conv_b.reshape(1, -1), wcat, bcat, lam)


def _merge_kernel(ya_ref, yb_ref, yc_ref, ml_ref, xs_ref, g1_ref, wa_ref, wb_ref, wc_ref, wo_ref, o_ref):
    d = D_MODEL
    acc = _sigmoid(ml_ref[:, 0:d]) * _dot(ya_ref[...], wa_ref[...])
    acc = acc + _sigmoid(ml_ref[:, d:2 * d]) * _dot(yb_ref[...], wb_ref[...])
    acc = acc + _sigmoid(ml_ref[:, 2 * d:3 * d]) * _dot(yc_ref[...], wc_ref[...])
    y = _dot(acc.astype(BF16), wo_ref[...])
    o_ref[...] = xs_ref[...] + g1_ref[...] * y


def _merge(ya, yb, yc, u, xs, mods, wa, wb, wc, wo, batch, n_ctx, drop_ctx):
    m, d = xs.shape
    tpb = m // batch // TM
    ctx_tiles = n_ctx // TM
    if drop_ctx:
        out_tpb = tpb - ctx_tiles
        src = lambda i: (i // out_tpb) * tpb + ctx_tiles + i % out_tpb
        row = lambda i: i // out_tpb
        m_out = batch * out_tpb * TM
    else:
        src = lambda i: i
        row = functools.partial(_mod_row, tiles_per_batch=tpb, ctx_tiles=ctx_tiles)
        m_out = m
    br = pl.BlockSpec((TM, HG_WIDTH), lambda i: (src(i), 0))
    full = lambda a: pl.BlockSpec(a.shape, lambda i: (0, 0))
    return pl.pallas_call(
        _merge_kernel,
        grid=(m_out // TM,),
        in_specs=[br, br, br,
                  pl.BlockSpec((TM, 3 * d), lambda i: (src(i), OFF_MERGE)),
                  pl.BlockSpec((TM, d), lambda i: (src(i), 0)),
                  pl.BlockSpec((None, 1, d), lambda i: (row(i), 0, 2)),
                  full(wa), full(wb), full(wc), full(wo)],
        out_specs=pl.BlockSpec((TM, d), lambda i: (i, 0)),
        out_shape=jax.ShapeDtypeStruct((m_out, d), F32),
        compiler_params=_params("parallel"),
        name="merge_out",
    )(ya, yb, yc, u, xs, mods, wa, wb, wc, wo)


def _swiglu_acc(hb, wg_ref, wu_ref, wd_ref):
    acc = jnp.zeros((hb.shape[0], wd_ref.shape[1]), F32)
    for c0 in range(0, D_FF, FF_CHUNK):
        a = _dot(hb, wg_ref[:, c0:c0 + FF_CHUNK])
        b = _dot(hb, wu_ref[:, c0:c0 + FF_CHUNK])
        acc = acc + _dot((_silu(a) * b).astype(BF16), wd_ref[c0:c0 + FF_CHUNK, :])
    return acc


def _final_norm(y, fw):
    ms = jnp.mean(y * y, axis=-1, keepdims=True)
    return y * lax.rsqrt(ms + EPS) * fw


def _ffn_kernel(x_ref, nw_ref, sc_ref, sh_ref, g2_ref, wg_ref, wu_ref, wd_ref, fw_ref, o_ref, *, final):
    x = x_ref[...]
    hb = _norm_mod(x, nw_ref[...], sc_ref[...], sh_ref[...]).astype(BF16)
    y = x + g2_ref[...] * _swiglu_acc(hb, wg_ref, wu_ref, wd_ref)
    o_ref[...] = _final_norm(y, fw_ref[...]) if final else y


def _mod_specs(row, d):
    return [pl.BlockSpec((None, 1, d), lambda i, *_: (row(i), 0, 4)),
            pl.BlockSpec((None, 1, d), lambda i, *_: (row(i), 0, 3)),
            pl.BlockSpec((None, 1, d), lambda i, *_: (row(i), 0, 5))]


def _ffn(xs, nw, mods, wg, wu, wd, fw, tiles_per_batch, ctx_tiles, final):
    m, d = xs.shape
    row = functools.partial(_mod_row, tiles_per_batch=tiles_per_batch, ctx_tiles=ctx_tiles)
    full = lambda a: pl.BlockSpec(a.shape, lambda i: (0, 0))
    vec = pl.BlockSpec((1, d), lambda i: (0, 0))
    return pl.pallas_call(
        functools.partial(_ffn_kernel, final=final),
        grid=(m // TM,),
        in_specs=[pl.BlockSpec((TM, d), lambda i: (i, 0)), vec] + _mod_specs(row, d)
                 + [full(wg), full(wu), full(wd), vec],
        out_specs=pl.BlockSpec((TM, d), lambda i: (i, 0)),
        out_shape=jax.ShapeDtypeStruct((m, d), F32),
        compiler_params=_params("parallel"),
        name="ffn_dense",
    )(xs, nw.reshape(1, d), mods, mods, mods, wg, wu, wd, fw.reshape(1, d))


def _top2(logits):
    lane = lax.broadcasted_iota(jnp.int32, logits.shape, 1)
    m1 = jnp.max(logits, axis=-1, keepdims=True)
    i1 = jnp.min(jnp.where(logits == m1, lane, LANES), axis=-1, keepdims=True)
    rest = jnp.where(lane == i1, -jnp.inf, logits)
    m2 = jnp.max(rest, axis=-1, keepdims=True)
    i2 = jnp.min(jnp.where(rest == m2, lane, LANES), axis=-1, keepdims=True)
    e2 = jnp.exp(m2 - m1)
    w1 = 1.0 / (1.0 + e2)
    return i1, i2, w1, e2 * w1


def _router_kernel(x_ref, nw_ref, sc_ref, sh_ref, rw_ref, meta_ref, wts_ref, cnt_ref, base_ref):
    @pl.when(pl.program_id(0) == 0)
    def _():
        base_ref[...] = jnp.zeros_like(base_ref)

    h = _norm_mod(x_ref[...], nw_ref[...], sc_ref[...], sh_ref[...])
    lane = lax.broadcasted_iota(jnp.int32, (h.shape[0], LANES), 1)
    logits = jnp.full((h.shape[0], LANES), -jnp.inf, F32)
    for e in range(N_EXPERTS):
        logits = jnp.where(lane == e, jnp.sum(h * rw_ref[e:e + 1, :], axis=-1, keepdims=True), logits)
    i1, i2, w1, w2 = _top2(logits)
    hit = jnp.where((lane == i1) | (lane == i2), 1.0, 0.0)
    tm = hit.shape[0]
    tri = lax.broadcasted_iota(jnp.int32, (tm, tm), 0) > lax.broadcasted_iota(jnp.int32, (tm, tm), 1)
    before = _dot(jnp.where(tri, 1.0, 0.0).astype(BF16), hit.astype(BF16)) + base_ref[...]
    r1 = jnp.sum(jnp.where(lane == i1, before, 0.0), axis=-1, keepdims=True).astype(jnp.int32)
    r2 = jnp.sum(jnp.where(lane == i2, before, 0.0), axis=-1, keepdims=True).astype(jnp.int32)
    base_ref[...] += jnp.sum(hit, axis=0, keepdims=True)
    cnt_ref[...] = base_ref[...]
    meta_ref[...] = jnp.where(lane == 0, i1, jnp.where(lane == 1, i2, jnp.where(lane == 2, r1,
                              jnp.where(lane == 3, r2, 0))))
    wts_ref[...] = jnp.where(lane == 0, w1, jnp.where(lane == 1, w2, 0.0))


def _router(xs, nw, mods, rw, row):
    m, d = xs.shape
    vec = pl.BlockSpec((1, d), lambda i: (0, 0))
    tok = pl.BlockSpec((TM, LANES), lambda i: (i, 0))
    return pl.pallas_call(
        _router_kernel,
        grid=(m // TM,),
        in_specs=[pl.BlockSpec((TM, d), lambda i: (i, 0)), vec] + _mod_specs(row, d)[:2]
                 + [pl.BlockSpec((N_EXPERTS, d), lambda i: (0, 0))],
        out_specs=[tok, tok, pl.BlockSpec((1, LANES), lambda i: (0, 0))],
        out_shape=[jax.ShapeDtypeStruct((m, LANES), jnp.int32), jax.ShapeDtypeStruct((m, LANES), F32),
                   jax.ShapeDtypeStruct((1, LANES), F32)],
        scratch_shapes=[pltpu.VMEM((1, LANES), F32)],
        compiler_params=_params("arbitrary"),
        name="moe_router",
    )(xs, nw.reshape(1, d), mods, mods, rw)


def _row_copy(src_ref, s, dst_ref, t, sem):
    return pltpu.make_async_copy(src_ref.at[pl.ds(s, 1)], dst_ref.at[pl.ds(t, 1)], sem)


def _dispatch_kernel(pos_ref, x_ref, nw_ref, sc_ref, sh_ref, xin_ref, xbuf_ref, h_ref, sem):
    del xin_ref
    h_ref[...] = _norm_mod(x_ref[...], nw_ref[...], sc_ref[...], sh_ref[...])
    tm = h_ref.shape[0]

    def issue(r, _):
        _row_copy(h_ref, r, xbuf_ref, pos_ref[0, 2 * r], sem).start(priority=0)
        _row_copy(h_ref, r, xbuf_ref, pos_ref[0, 2 * r + 1], sem).start(priority=1)
        return 0

    lax.fori_loop(0, tm, issue, 0, unroll=DMA_UNROLL)

    def drain(r, _):
        _row_copy(h_ref, 0, xbuf_ref, 0, sem).wait()
        _row_copy(h_ref, 0, xbuf_ref, 0, sem).wait()
        return 0

    lax.fori_loop(0, tm, drain, 0, unroll=DMA_UNROLL)


def _dispatch(xs, nw, mods, pos, row, n_rows):
    m, d = xs.shape
    vec = pl.BlockSpec((1, d), lambda i: (0, 0))
    slab = (d,)
    zeros = jnp.zeros((n_rows,) + slab, F32)
    return pl.pallas_call(
        _dispatch_kernel,
        grid=(m // TM,),
        in_specs=[pl.BlockSpec((None, 1, 2 * TM), lambda i: (i, 0, 0), memory_space=pltpu.SMEM),
                  pl.BlockSpec((TM, d), lambda i: (i, 0)), vec] + _mod_specs(row, d)[:2]
                 + [pl.BlockSpec(memory_space=pl.ANY)],
        out_specs=pl.BlockSpec(memory_space=pl.ANY),
        out_shape=jax.ShapeDtypeStruct((n_rows,) + slab, F32),
        scratch_shapes=[pltpu.VMEM((TM,) + slab, F32), pltpu.SemaphoreType.DMA(())],
        input_output_aliases={5: 0},
        compiler_params=_params("arbitrary"),
        name="moe_dispatch",
    )(pos, xs, nw.reshape(1, d), mods, mods, zeros)


def _expert_ffn_kernel(te_ref, nu_ref, x_ref, wg_ref, wu_ref, wd_ref, y_ref):
    del te_ref
    used = pl.program_id(0) < nu_ref[0]

    @pl.when(used)
    def _():
        y_ref[...] = _swiglu_acc(x_ref[...].astype(BF16), wg_ref, wu_ref, wd_ref)

    @pl.when(jnp.logical_not(used))
    def _():
        y_ref[...] = jnp.zeros_like(y_ref)


def _expert_ffn(xbuf, tile_expert, n_used, wg, wu, wd):
    n_rows = xbuf.shape[0]
    d = wg.shape[1]
    wspec = lambda shape: pl.BlockSpec((None,) + shape, lambda i, te, nu: (te[i], 0, 0))
    tok = pl.BlockSpec((TM, d), lambda i, te, nu: (i, 0))
    return pl.pallas_call(
        _expert_ffn_kernel,
        grid_spec=pltpu.PrefetchScalarGridSpec(
            num_scalar_prefetch=2,
            grid=(n_rows // TM,),
            in_specs=[tok, wspec((d, D_FF)), wspec((d, D_FF)), wspec((D_FF, d))],
            out_specs=tok,
        ),
        out_shape=jax.ShapeDtypeStruct(xbuf.shape, F32),
        compiler_params=_params("arbitrary"),
        name="moe_expert_ffn",
    )(tile_expert, n_used, xbuf, wg, wu, wd)


def _combine_kernel(pos_ref, x_ref, wts_ref, g2_ref, fw_ref, y_ref, o_ref, buf_ref, sem, *, final):
    tm = x_ref.shape[0]

    def issue(r, _):
        _row_copy(y_ref, pos_ref[0, 2 * r], buf_ref.at[0], r, sem).start(priority=0)
        _row_copy(y_ref, pos_ref[0, 2 * r + 1], buf_ref.at[1], r, sem).start(priority=1)
        return 0

    lax.fori_loop(0, tm, issue, 0, unroll=DMA_UNROLL)

    def drain(r, _):
        _row_copy(y_ref, 0, buf_ref.at[0], 0, sem).wait()
        _row_copy(y_ref, 0, buf_ref.at[0], 0, sem).wait()
        return 0

    lax.fori_loop(0, tm, drain, 0, unroll=DMA_UNROLL)
    w = wts_ref[...]
    f = w[:, 0:1] * buf_ref[0] + w[:, 1:2] * buf_ref[1]
    y = x_ref[...] + g2_ref[...] * f
    o_ref[...] = _final_norm(y, fw_ref[...]) if final else y


def _combine(xs, wts, mods, fw, ybuf, pos, row, final):
    m, d = xs.shape
    vec = pl.BlockSpec((1, d), lambda i: (0, 0))
    return pl.pallas_call(
        functools.partial(_combine_kernel, final=final),
        grid=(m // TM,),
        in_specs=[pl.BlockSpec((None, 1, 2 * TM), lambda i: (i, 0, 0), memory_space=pltpu.SMEM),
                  pl.BlockSpec((TM, d), lambda i: (i, 0)),
                  pl.BlockSpec((TM, LANES), lambda i: (i, 0)),
                  _mod_specs(row, d)[2], vec,
                  pl.BlockSpec(memory_space=pl.ANY)],
        out_specs=pl.BlockSpec((TM, d), lambda i: (i, 0)),
        out_shape=jax.ShapeDtypeStruct((m, d), F32),
        scratch_shapes=[pltpu.VMEM((2, TM) + ybuf.shape[1:], F32), pltpu.SemaphoreType.DMA(())],
        compiler_params=_params("arbitrary"),
        name="moe_combine",
    )(pos, xs, wts, mods, fw.reshape(1, d), ybuf)


def _moe(xs, nw, mods, rw, wg, wu, wd, fw, tiles_per_batch, ctx_tiles, final):
    m, d = xs.shape
    row = functools.partial(_mod_row, tiles_per_batch=tiles_per_batch, ctx_tiles=ctx_tiles)
    meta, wts, cnt = _router(xs, nw, mods, rw, row)
    counts = cnt[0, :N_EXPERTS].astype(jnp.int32)
    padded = (counts + TM - 1) // TM * TM
    ends = jnp.cumsum(padded)
    starts = ends - padded
    pos = jnp.stack([starts[meta[:, 0]] + meta[:, 2], starts[meta[:, 1]] + meta[:, 3]], axis=-1)
    pos = pos.reshape(m // TM, 1, 2 * TM)
    n_rows = 2 * m + N_EXPERTS * TM
    n_used = (ends[-1] // TM).astype(jnp.int32)
    tile_start = jnp.arange(n_rows // TM, dtype=jnp.int32) * TM
    tile_row = jnp.minimum(tile_start, ends[-1] - 1)
    tile_expert = jnp.sum((tile_row[:, None] >= ends[None, :]).astype(jnp.int32), axis=1)
    xbuf = _dispatch(xs, nw, mods, pos, row, n_rows)
    ybuf = _expert_ffn(xbuf, tile_expert, n_used.reshape(1), wg, wu, wd)
    return _combine(xs, wts, mods, fw, ybuf, pos, row, final)


def _cast_kernel(x_ref, o_ref):
    o_ref[...] = x_ref[...].astype(o_ref.dtype)


def _to_bf16(w, layer):
    cols = w.shape[-1]
    w2 = w.reshape(-1, cols)
    rows = TM * D_FF // cols
    blocks = w2.shape[0] // w.shape[0] // rows
    out = pl.pallas_call(
        _cast_kernel,
        grid=(blocks,),
        in_specs=[pl.BlockSpec((rows, cols), lambda i: (layer * blocks + i, 0))],
        out_specs=pl.BlockSpec((rows, cols), lambda i: (i, 0)),
        out_shape=jax.ShapeDtypeStruct((blocks * rows, cols), BF16),
        compiler_params=_params("parallel"),
        name="weights_to_bf16",
    )(w2)
    return out.reshape(w.shape[1:])


def _column_perm():
    sizes = (512, 512, 512, 512, 512, 512, 128, 128, 512, 512, 3 * D_MODEL)
    starts = np.concatenate([[0], np.cumsum(sizes)[:-1]])
    a_q, a_ff, a_fb, a_v, a_g, b_q, b_k, b_v, c_x, c_g, mrg = [
        np.arange(s, s + n) for s, n in zip(starts, sizes)]
    half = np.concatenate([np.arange(0, HEAD_DIM, 2), np.arange(1, HEAD_DIM, 2)])
    group = ATT_HEADS // ATT_KV_HEADS
    q_cols = []
    for j in range(group):
        q_cols += [b_q[j * HEAD_DIM + half], b_q[(group + j) * HEAD_DIM + half]]
    k_cols = [b_k[h * HEAD_DIM + half] for h in range(ATT_KV_HEADS)]
    return np.concatenate([mrg, a_q, a_ff, a_fb, a_v, a_g, c_x, c_g] + q_cols + k_cols + [b_v])


def _permute_columns(w, perm):
    cuts = np.flatnonzero(np.diff(perm) != 1) + 1
    parts, pending = [], []
    for run in np.split(perm, cuts):
        if len(run) >= LANES:
            if pending:
                parts.append(w[:, np.concatenate(pending)])
                pending = []
            parts.append(w[:, int(run[0]):int(run[-1]) + 1])
        else:
            pending.append(run)
    if pending:
        parts.append(w[:, np.concatenate(pending)])
    return jnp.concatenate(parts, axis=1)


def _att_out_rows():
    group = ATT_HEADS // ATT_KV_HEADS
    idx = []
    for j in range(group):
        idx += [np.arange(j * HEAD_DIM, (j + 1) * HEAD_DIM),
                np.arange((group + j) * HEAD_DIM, (group + j + 1) * HEAD_DIM)]
    return np.concatenate(idx)


def _rope_tables(n_ctx, n_lat):
    rows = n_lat // GRID_W
    row = jnp.repeat(jnp.arange(rows, dtype=F32), GRID_W)
    col = jnp.tile(jnp.arange(GRID_W, dtype=F32), rows)
    pairs = HEAD_DIM // 4
    freqs = ROPE_THETA ** (-jnp.arange(pairs, dtype=F32) / pairs)
    ang = jnp.concatenate([row[:, None] * freqs, col[:, None] * freqs], axis=-1)
    cos, sin = jnp.cos(ang), jnp.sin(ang)
    cos = jnp.concatenate([jnp.ones((n_ctx, HEAD_DIM // 2), F32), cos], axis=0)
    sin = jnp.concatenate([jnp.zeros((n_ctx, HEAD_DIM // 2), F32), sin], axis=0)
    reps = LANES // HEAD_DIM
    return (jnp.tile(jnp.concatenate([cos, cos], axis=-1), (1, reps)),
            jnp.tile(jnp.concatenate([-sin, sin], axis=-1), (1, reps)))


def _lru_weights(wa, wx, ba, bx):
    ncg = LRU_WIDTH // LANES

    def dense(w):
        w = w.reshape(ncg, 2, LRU_BLOCK_W, LRU_BLOCK_W)
        out = jnp.zeros((ncg, LANES, LANES), w.dtype)
        out = out.at[:, :LRU_BLOCK_W, :LRU_BLOCK_W].set(w[:, 0])
        return out.at[:, LRU_BLOCK_W:, LRU_BLOCK_W:].set(w[:, 1])

    wcat = jnp.concatenate([dense(wa[0]), dense(wx[0]), dense(wa[1]), dense(wx[1])], axis=-1).astype(BF16)
    bcat = jnp.concatenate([b.reshape(ncg, 1, LANES) for b in (ba[0], bx[0], ba[1], bx[1])], axis=-1)
    return wcat, bcat


def kernel(x, c, ctx, c_ctx, ada_w, ada_b, mix_norm_w, ffn_norm_w, w_in, hg_lb_logits, hg_norm_w, q_norm_w, k_norm_w, lru_conv_w, lru_conv_b, lru_wa, lru_ba, lru_wx, lru_bx, lru_lambda, w_br_a, w_br_b, w_br_c, w_out, ffn_w_gate, ffn_w_up, ffn_w_down, router_w, moe_w_gate, moe_w_up, moe_w_down, final_norm_w):
    batch, n_lat, d = x.shape
    n_ctx = ctx.shape[1]
    t = n_ctx + n_lat
    depth = ada_w.shape[0]
    assert n_ctx % TM == 0 and n_lat % TM == 0 and batch <= 4

    cvec = jnp.zeros((SUBLANES, d), F32).at[:batch].set(c).at[4].set(c_ctx)
    mods_all = _ada_mods(cvec, ada_w, ada_b).reshape(depth, SUBLANES, 1, 6 * d)

    p = jax.nn.softmax(hg_lb_logits.astype(F32), axis=0)
    cum = jnp.cumsum(p, axis=0)
    lbs = cum - cum[:1]
    log_lb, log_1m_lb = jnp.log(lbs) * LOG2E, jnp.log1p(-lbs) * LOG2E

    cos, sin = _rope_tables(n_ctx, n_lat)
    half = np.concatenate([np.arange(0, HEAD_DIM, 2), np.arange(1, HEAD_DIM, 2)])
    gidx = np.arange(LANES) // HEAD_DIM
    gmat = jnp.asarray(gidx[:, None] == gidx[None, :], BF16)
    col_perm = _column_perm()
    att_rows = _att_out_rows()

    xs = jnp.concatenate([ctx.astype(x.dtype), x], axis=1).reshape(batch * t, d)
    tpb, ctx_tiles = t // TM, n_ctx // TM
    for i in range(depth):
        last = i == depth - 1
        mods = mods_all[i]
        w_i = _permute_columns(w_in[i], col_perm).astype(BF16)
        u = _in_proj(xs, mix_norm_w[i], mods, w_i, tpb, ctx_tiles)

        o_f = _hg_direction(u, log_lb[i, 0], log_1m_lb[i, 0], batch, n_ctx, False)
        y_a = _hg_direction(u, log_lb[i, 1], log_1m_lb[i, 1], batch, n_ctx, True, o_f, hg_norm_w[i])

        reps = LANES // HEAD_DIM
        qw = jnp.tile(q_norm_w[i][half], reps).reshape(1, LANES)
        kw = jnp.tile(k_norm_w[i][half], reps).reshape(1, LANES)
        qr, k0, k1, v0, v1 = _att_prep(u, cos, sin, qw, kw, gmat, batch)
        bound = (1.02 * HEAD_DIM ** 0.5 * float(np.log2(np.e))
                 * jnp.max(jnp.abs(q_norm_w[i])) * jnp.max(jnp.abs(k_norm_w[i]))).reshape(1, 1)
        y_b = _attention(bound, qr, k0, k1, v0, v1, batch, n_ctx)

        wcat, bcat = _lru_weights(lru_wa[i], lru_wx[i], lru_ba[i], lru_bx[i])
        y_c = _rglru(u, lru_conv_w[i], lru_conv_b[i], wcat, bcat, lru_lambda[i], batch, n_ctx)

        xs = _merge(y_a, y_b, y_c, u, xs, mods, w_br_a[i].astype(BF16), w_br_b[i][att_rows].astype(BF16),
                    w_br_c[i].astype(BF16), w_out[i].astype(BF16), batch, n_ctx, last)
        if last:
            tpb, ctx_tiles = n_lat // TM, 0
        if i % 2 == 0:
            j = i // 2
            xs = _ffn(xs, ffn_norm_w[i], mods, _to_bf16(ffn_w_gate, j), _to_bf16(ffn_w_up, j),
                      _to_bf16(ffn_w_down, j), final_norm_w, tpb, ctx_tiles, last)
        else:
            j = i // 2
            rw = router_w[j].T
            xs = _moe(xs, ffn_norm_w[i], mods, rw, _to_bf16(moe_w_gate, j), _to_bf16(moe_w_up, j),
                      _to_bf16(moe_w_down, j), final_norm_w, tpb, ctx_tiles, last)
    return xs.reshape(batch, n_lat, d)
```

```python
import functools

import numpy as np
import jax
import jax.numpy as jnp
from jax import lax
from jax.experimental import pallas as pl
from jax.experimental.pallas import tpu as pltpu

F32 = jnp.float32
BF16 = jnp.bfloat16

D_MODEL = 1024
DEPTH = 4
GRID_W = 64
HG_HEADS = 4
HG_DK = 128
HG_WIDTH = 512
ATT_HEADS = 8
ATT_KV_HEADS = 2
HEAD_DIM = 64
ATT_WIDTH = 512
ROPE_THETA = 10000.0
LRU_WIDTH = 512
LRU_BLOCK_W = 64
CONV_W = 4
RG_C = 8.0
D_FF = 2816
N_EXPERTS = 8
EPS = 1e-6

LOG2E = float(np.log2(np.e))
LANES = 128
SUBLANES = 8
TM = 256
HG_CHUNK = 64
HG_SUB = 8
ATT_TQ = 512
ATT_SAFE_BOUND = 60.0
ATT_VT_ROWS = 80
FF_CHUNK = 2816
IN_PROJ_CHUNK = 512
DMA_UNROLL = 8
VMEM_LIMIT = 56 * 1024 * 1024

OFF_MERGE = 0
OFF_HG = 3 * D_MODEL
OFF_LRU = OFF_HG + 5 * HG_WIDTH
OFF_ATT_Q = OFF_LRU + 2 * LRU_WIDTH
OFF_ATT_K = OFF_ATT_Q + ATT_WIDTH
OFF_ATT_V = OFF_ATT_K + ATT_KV_HEADS * HEAD_DIM
IN_DIM = OFF_ATT_V + ATT_KV_HEADS * HEAD_DIM


def _dot(a, b):
    return jnp.dot(a, b, preferred_element_type=F32)


def _dot_nt(a, b):
    return lax.dot_general(a, b, (((1,), (1,)), ((), ())), preferred_element_type=F32)


def _dot_tn(a, b):
    return lax.dot_general(a, b, (((0,), (0,)), ((), ())), preferred_element_type=F32)


def _sigmoid(x):
    return 1.0 / (1.0 + jnp.exp(-x))


def _silu(x):
    return x * _sigmoid(x)


def _params(*sem):
    return pltpu.CompilerParams(dimension_semantics=sem, vmem_limit_bytes=VMEM_LIMIT)


def _mod_row(i, tiles_per_batch, ctx_tiles):
    if ctx_tiles == 0:
        return i // tiles_per_batch
    return jnp.where(i % tiles_per_batch < ctx_tiles, 4, i // tiles_per_batch)


def _norm_mod(x, nw, sc, sh):
    ms = jnp.mean(x * x, axis=-1, keepdims=True)
    return (x * lax.rsqrt(ms + EPS) * nw) * (1.0 + sc) + sh


def _ada_kernel(c_ref, w_ref, b_ref, o_ref):
    s = _silu(c_ref[...])
    o_ref[...] = _dot(s.astype(BF16), w_ref[...].astype(BF16)) + b_ref[...]


def _ada_mods(cvec, ada_w, ada_b):
    depth, d, n = ada_w.shape
    tn = 1536
    return pl.pallas_call(
        _ada_kernel,
        grid=(depth, n // tn),
        in_specs=[
            pl.BlockSpec((SUBLANES, d), lambda l, j: (0, 0)),
            pl.BlockSpec((None, d, tn), lambda l, j: (l, 0, j)),
            pl.BlockSpec((None, 1, tn), lambda l, j: (l, 0, j)),
        ],
        out_specs=pl.BlockSpec((None, SUBLANES, tn), lambda l, j: (l, 0, j)),
        out_shape=jax.ShapeDtypeStruct((depth, SUBLANES, n), F32),
        compiler_params=_params("parallel", "parallel"),
        name="ada_mods",
    )(cvec, ada_w, ada_b.reshape(depth, 1, n))


def _in_proj_kernel(x_ref, nw_ref, sc_ref, sh_ref, w_ref, o_ref):
    hb = _norm_mod(x_ref[...], nw_ref[...], sc_ref[...], sh_ref[...]).astype(BF16)
    n = w_ref.shape[1]
    step = IN_PROJ_CHUNK
    for c0 in range(0, n, step):
        cw = min(step, n - c0)
        o_ref[:, c0:c0 + cw] = _dot(hb, w_ref[:, c0:c0 + cw]).astype(o_ref.dtype)


def _in_proj(xs, nw, mods, w, tiles_per_batch, ctx_tiles):
    m, d = xs.shape
    n = w.shape[1]
    row = functools.partial(_mod_row, tiles_per_batch=tiles_per_batch, ctx_tiles=ctx_tiles)
    return pl.pallas_call(
        _in_proj_kernel,
        grid=(m // TM,),
        in_specs=[
            pl.BlockSpec((TM, d), lambda i: (i, 0)),
            pl.BlockSpec((1, d), lambda i: (0, 0)),
            pl.BlockSpec((None, 1, d), lambda i: (row(i), 0, 1)),
            pl.BlockSpec((None, 1, d), lambda i: (row(i), 0, 0)),
            pl.BlockSpec((d, n), lambda i: (0, 0)),
        ],
        out_specs=pl.BlockSpec((TM, n), lambda i: (i, 0)),
        out_shape=jax.ShapeDtypeStruct((m, n), F32),
        compiler_params=_params("parallel"),
        name="in_proj",
    )(xs, nw.reshape(1, d), mods, mods, w)


def _cumsum_rows(x, reverse):
    n = x.shape[0]
    r = lax.broadcasted_iota(jnp.int32, (n, n), 0)
    c = lax.broadcasted_iota(jnp.int32, (n, n), 1)
    tri = jnp.where((c >= r) if reverse else (c <= r), 1.0, 0.0).astype(BF16)
    p1 = x.astype(BF16)
    rest = x - p1.astype(F32)
    p2 = rest.astype(BF16)
    p3 = (rest - p2.astype(F32)).astype(BF16)
    return _dot(tri, p1) + _dot(tri, p2) + _dot(tri, p3)


def _bcast_row(x, r, rows):
    return jnp.broadcast_to(x[r:r + 1, :], (rows, x.shape[1]))


def _hg_scores(qh, kh, vh, cum, st, reverse, masks):
    c = qh.shape[0]
    last = cum[0:1, :] if reverse else cum[c - 1:c, :]
    o = _dot_nt((qh * jnp.exp2(cum)).astype(BF16), st.astype(BF16))
    kd = kh * jnp.exp2(last - cum)
    st_new = st * jnp.exp2(last) + _dot_tn(vh.astype(BF16), kd.astype(BF16))

    a = jnp.zeros((c, c), F32)
    half = c // 2
    li = 0
    while half >= HG_SUB:
        blk = 2 * half
        refs = []
        for b0 in range(0, c, blk):
            r = b0 + half if reverse else b0 + half - 1
            refs.append(_bcast_row(cum, r, blk))
        ref = refs[0] if len(refs) == 1 else jnp.concatenate(refs, axis=0)
        qs = qh * jnp.exp2(cum - ref)
        ks = kh * jnp.exp2(ref - cum)
        a = a + jnp.where(masks[li], _dot_nt(qs.astype(BF16), ks.astype(BF16)), 0.0)
        half //= 2
        li += 1

    pieces = []
    for g in range(c // HG_SUB):
        r0 = g * HG_SUB
        qg = qh[r0:r0 + HG_SUB, :]
        cg = cum[r0:r0 + HG_SUB, :]
        for j in range(HG_SUB):
            pieces.append(qg * jnp.exp2(cg - _bcast_row(cum, r0 + j, HG_SUB)))
    rs = _dot_nt(jnp.concatenate(pieces, axis=0).astype(BF16), kh.astype(BF16))
    return o, st_new, a, rs


def _hg_finish(o, a, rs, vh, reverse):
    c = vh.shape[0]
    lane = lax.broadcasted_iota(jnp.int32, (HG_SUB, c), 1)
    rsub = lax.broadcasted_iota(jnp.int32, (HG_SUB, c), 0)
    diag = []
    for g in range(c // HG_SUB):
        acc = jnp.zeros((HG_SUB, c), F32)
        for j in range(HG_SUB):
            idx = g * HG_SUB + j
            causal = (rsub <= j) if reverse else (rsub >= j)
            acc = jnp.where((lane == idx) & causal, rs[idx * HG_SUB:(idx + 1) * HG_SUB, :], acc)
        diag.append(acc)
    a = a + jnp.concatenate(diag, axis=0)
    return o + _dot(a.astype(BF16), vh.astype(BF16))


def _hg_kernel(*refs, reverse, final, rows):
    if final:
        q_ref, f_ref, v_ref, la_ref, lb_ref, g_ref, op_ref, nw_ref, o_ref, s_ref = refs
    else:
        q_ref, f_ref, v_ref, la_ref, lb_ref, o_ref, s_ref = refs

    @pl.when(pl.program_id(1) == 0)
    def _():
        s_ref[...] = jnp.zeros_like(s_ref)

    c = HG_CHUNK
    ti = lax.broadcasted_iota(jnp.int32, (c, c), 0)
    si = lax.broadcasted_iota(jnp.int32, (c, c), 1)
    masks = []
    half = c // 2
    while half >= HG_SUB:
        blk = 2 * half
        same = (ti & -blk) == (si & -blk)
        if reverse:
            m = same & ((ti & half) == 0) & ((si & half) != 0)
        else:
            m = same & ((ti & half) != 0) & ((si & half) == 0)
        masks.append(m)
        half //= 2

    n_ch = rows // c
    order = range(n_ch - 1, -1, -1) if reverse else range(n_ch)
    gated = []
    for ci in order:
        r0 = ci * c
        z = f_ref[r0:r0 + c, :]
        z = z * LOG2E
        ls = jnp.minimum(z, 0.0) - jnp.log2(1.0 + jnp.exp2(-jnp.abs(z)))
        bt = lb_ref[...] + ls
        at = jnp.broadcast_to(la_ref[...], bt.shape)
        logf = jnp.maximum(at, bt) + jnp.log2(1.0 + jnp.exp2(-jnp.abs(at - bt)))
        gated.append((r0, _silu(q_ref[r0:r0 + c, :]), 1.0 - jnp.exp2(logf), v_ref[r0:r0 + c, :],
                      _cumsum_rows(logf, reverse)))

    states = [s_ref[hh] for hh in range(HG_HEADS)]
    heads = [slice(hh * HG_DK, (hh + 1) * HG_DK) for hh in range(HG_HEADS)]
    def finish(r0, vv, partial):
        outs = []
        for (o, a, rs), sl in zip(partial, heads):
            o = _hg_finish(o, a, rs, vv[:, sl], reverse)
            if final:
                tot = o + op_ref[r0:r0 + c, sl]
                ms = jnp.mean(tot * tot, axis=-1, keepdims=True)
                o = tot * lax.rsqrt(ms + EPS) * nw_ref[:, sl]
            outs.append(o)
        o_all = jnp.concatenate(outs, axis=1)
        if final:
            o_all = o_all * _silu(g_ref[r0:r0 + c, :])
        return r0, o_all.astype(o_ref.dtype)

    pending, done = None, []
    for r0, qf, kf, vv, cum in gated:
        partial = []
        for hh, sl in enumerate(heads):
            o, states[hh], a, rs = _hg_scores(qf[:, sl], kf[:, sl], vv[:, sl], cum[:, sl], states[hh],
                                              reverse, masks)
            partial.append((o, a, rs))
        if pending is not None:
            done.append(finish(*pending))
        pending = (r0, vv, partial)
    done.append(finish(*pending))
    for r0, o_all in done:
        o_ref[r0:r0 + c, :] = o_all
    for hh in range(HG_HEADS):
        s_ref[hh] = states[hh]


def _hg_direction(u, la, lb, batch, n_ctx, reverse, o_prev=None, nw=None):
    m = u.shape[0]
    t = m // batch
    nblk = t // TM
    ncb = n_ctx // TM
    final = o_prev is not None
    cb = OFF_HG // HG_WIDTH

    def blk(j):
        if not reverse:
            return j
        return jnp.where(j < ncb, ncb - 1 - j, nblk - 1 - (j - ncb))

    def tok(col):
        return pl.BlockSpec((TM, HG_WIDTH), lambda b, j: (b * nblk + blk(j), col))

    vec = pl.BlockSpec((1, HG_WIDTH), lambda b, j: (0, 0))
    f_col = cb + 2 if reverse else cb + 1
    in_specs = [tok(cb), tok(f_col), tok(cb + 3), vec, vec]
    args = [u, u, u, la.reshape(1, -1), lb.reshape(1, -1)]
    if final:
        in_specs += [tok(cb + 4), tok(0), vec]
        args += [u, o_prev, nw.reshape(1, -1)]
    return pl.pallas_call(
        functools.partial(_hg_kernel, reverse=reverse, final=final, rows=TM),
        grid=(batch, nblk),
        in_specs=in_specs,
        out_specs=tok(0),
        out_shape=jax.ShapeDtypeStruct((m, HG_WIDTH), BF16 if final else F32),
        scratch_shapes=[pltpu.VMEM((HG_HEADS, HG_DK, HG_DK), F32)],
        compiler_params=_params("parallel", "arbitrary"),
        name="hgrn2_bwd" if reverse else "hgrn2_fwd",
    )(*args)


def _swap_halves(x):
    lane = lax.broadcasted_iota(jnp.int32, x.shape, 1)
    return jnp.where((lane & 32) == 0, pltpu.roll(x, LANES - 32, 1), pltpu.roll(x, 32, 1))


def _head_norm_rope(x, w, cos, sin, gmat):
    x2 = x * x
    hi = x2.astype(BF16)
    lo = (x2 - hi.astype(F32)).astype(BF16)
    ms = (_dot(hi, gmat) + _dot(lo, gmat)) * (1.0 / HEAD_DIM)
    xn = x * lax.rsqrt(ms + EPS) * w
    return xn * cos + _swap_halves(xn) * sin


def _att_prep_kernel(q_ref, k_ref, v_ref, cos_ref, sin_ref, qw_ref, kw_ref, g_ref,
                     qo_ref, k0_ref, k1_ref, v0_ref, v1_ref):
    cos, sin, gmat = cos_ref[...], sin_ref[...], g_ref[...]
    qscale = (HEAD_DIM ** -0.5) * float(np.log2(np.e))
    for j in range(ATT_WIDTH // LANES):
        sl = slice(j * LANES, (j + 1) * LANES)
        qr = _head_norm_rope(q_ref[:, sl], qw_ref[...], cos, sin, gmat)
        qo_ref[:, sl] = (qr * qscale).astype(BF16)
    kr = _head_norm_rope(k_ref[...], kw_ref[...], cos, sin, gmat)
    lane = lax.broadcasted_iota(jnp.int32, kr.shape, 1)
    k0_ref[...] = jnp.where(lane < HEAD_DIM, kr, 0.0).astype(BF16)
    k1_ref[...] = jnp.where(lane >= HEAD_DIM, kr, 0.0).astype(BF16)
    vt = jnp.transpose(v_ref[...])
    tail = jnp.where(lax.broadcasted_iota(jnp.int32, (ATT_VT_ROWS - HEAD_DIM, vt.shape[1]), 0) == 0, 1.0, 0.0)
    v0_ref[...] = jnp.concatenate([vt[:HEAD_DIM], tail], axis=0).astype(BF16)
    v1_ref[...] = jnp.concatenate([vt[HEAD_DIM:], tail], axis=0).astype(BF16)


def _att_prep(u, cos, sin, qw, kw, gmat, batch):
    m = u.shape[0]
    tpb = m // batch // TM
    kvw = ATT_KV_HEADS * HEAD_DIM
    tab = pl.BlockSpec((TM, LANES), lambda i: (i % tpb, 0))
    vec = pl.BlockSpec((1, LANES), lambda i: (0, 0))
    kv_out = pl.BlockSpec((TM, kvw), lambda i: (i, 0))
    vt_out = pl.BlockSpec((ATT_VT_ROWS, TM), lambda i: (0, i))
    return pl.pallas_call(
        _att_prep_kernel,
        grid=(m // TM,),
        in_specs=[
            pl.BlockSpec((TM, ATT_WIDTH), lambda i: (i, OFF_ATT_Q // ATT_WIDTH)),
            pl.BlockSpec((TM, kvw), lambda i: (i, OFF_ATT_K // kvw)),
            pl.BlockSpec((TM, kvw), lambda i: (i, OFF_ATT_V // kvw)),
            tab, tab, vec, vec,
            pl.BlockSpec((LANES, LANES), lambda i: (0, 0)),
        ],
        out_specs=[pl.BlockSpec((TM, ATT_WIDTH), lambda i: (i, 0)), kv_out, kv_out, vt_out, vt_out],
        out_shape=[jax.ShapeDtypeStruct((m, ATT_WIDTH), BF16)] + [jax.ShapeDtypeStruct((m, kvw), BF16)] * 2
                  + [jax.ShapeDtypeStruct((ATT_VT_ROWS, m), BF16)] * 2,
        compiler_params=_params("parallel"),
        name="att_prep",
    )(u, u, u, cos, sin, qw, kw, gmat)


def _att_kernel(bound_ref, q_ref, k0_ref, k1_ref, v0_ref, v1_ref, o_ref, *, n_ctx, n_all, tq):
    bound = bound_ref[0, 0]

    def attend(r0, rows, n_keys, use_bound):
        q = q_ref[pl.ds(r0, rows), :]
        outs = []
        for k_ref, vt_ref in ((k0_ref, v0_ref), (k1_ref, v1_ref)):
            s = _dot_nt(k_ref[0:n_keys, :], q)
            shift = bound if use_bound else jnp.max(s, axis=0, keepdims=True)
            acc = _dot(vt_ref[:, 0:n_keys], jnp.exp2(s - shift).astype(BF16))
            outs.append(acc[:HEAD_DIM, :] / acc[HEAD_DIM:HEAD_DIM + 1, :])
        out = jnp.concatenate(outs, axis=0)
        o_ref[pl.ds(r0, rows), :] = jnp.transpose(out).astype(o_ref.dtype)

    def all_rows(use_bound):
        if n_ctx:
            attend(0, n_ctx, n_ctx, use_bound)

        def lat_chunk(i, _):
            attend(pl.multiple_of(n_ctx + i * tq, TM), tq, n_all, use_bound)
            return 0

        lax.fori_loop(0, (n_all - n_ctx) // tq, lat_chunk, 0)

    safe = bound <= ATT_SAFE_BOUND

    @pl.when(safe)
    def _():
        all_rows(True)

    @pl.when(jnp.logical_not(safe))
    def _():
        all_rows(False)


def _attention(bound, qr, k0, k1, v0, v1, batch, n_ctx):
    m = qr.shape[0]
    t = m // batch
    kvw = ATT_KV_HEADS * HEAD_DIM
    tq = ATT_TQ if (t - n_ctx) % ATT_TQ == 0 else TM
    qspec = pl.BlockSpec((t, LANES), lambda b, j: (b, j))
    kvspec = pl.BlockSpec((t, kvw), lambda b, j: (b, 0))
    vtspec = pl.BlockSpec((ATT_VT_ROWS, t), lambda b, j: (0, b))
    return pl.pallas_call(
        functools.partial(_att_kernel, n_ctx=n_ctx, n_all=t, tq=tq),
        grid=(batch, ATT_WIDTH // LANES),
        in_specs=[pl.BlockSpec(memory_space=pltpu.SMEM), qspec, kvspec, kvspec, vtspec, vtspec],
        out_specs=qspec,
        out_shape=jax.ShapeDtypeStruct((m, ATT_WIDTH), BF16),
        compiler_params=_params("parallel", "parallel"),
        name="attention",
    )(bound, qr, k0, k1, v0, v1)


def _scan_steps(a, u, reverse, axis):
    n = a.shape[axis]
    pos = lax.broadcasted_iota(jnp.int32, a.shape, axis)
    s = 1
    while s < n:
        if reverse:
            keep = pos < n - s
            ash, ush = pltpu.roll(a, n - s, axis), pltpu.roll(u, n - s, axis)
        else:
            keep = pos >= s
            ash, ush = pltpu.roll(a, s, axis), pltpu.roll(u, s, axis)
        u = u + a * jnp.where(keep, ush, 0.0)
        a = a * jnp.where(keep, ash, 1.0)
        s *= 2
    return a, u


def _scan_rows(a, u, carry, reverse, sa_ref, su_ref):
    n, width = a.shape
    groups = n // SUBLANES
    a3, u3 = _scan_steps(a.reshape(groups, SUBLANES, width), u.reshape(groups, SUBLANES, width), reverse, 1)
    a, u = a3.reshape(n, width), u3.reshape(n, width)
    sa_ref[...] = a
    su_ref[...] = u
    edge = 0 if reverse else SUBLANES - 1
    ae = sa_ref[pl.ds(edge, groups, stride=SUBLANES), :]
    ue = su_ref[pl.ds(edge, groups, stride=SUBLANES), :]
    ae, ue = _scan_steps(ae, ue, reverse, 0)
    hc = ue + ae * carry
    grow = lax.broadcasted_iota(jnp.int32, hc.shape, 0)
    if reverse:
        cin = jnp.where(grow < groups - 1, pltpu.roll(hc, groups - 1, 0), carry)
        out = hc[0:1, :]
    else:
        cin = jnp.where(grow >= 1, pltpu.roll(hc, 1, 0), carry)
        out = hc[groups - 1:groups, :]
    cin_rows = jnp.concatenate([_bcast_row(cin, i, SUBLANES) for i in range(groups)], axis=0)
    return u + a * cin_rows, out


def _lru_kernel(x_ref, g_ref, cw_ref, cb_ref, w_ref, b_ref, lam_ref, o_ref, xp_ref, xc_ref, h_ref,
                sa_ref, su_ref, *, n_ctx, n_tok):
    tt = TM
    pad = SUBLANES
    n_lat = n_tok - n_ctx
    lat0 = n_ctx + 2 * pad
    zeros = jnp.zeros((pad, LANES), F32)
    xp_ref[0:pad, :] = zeros
    xp_ref[pad + n_ctx:lat0, :] = zeros
    xp_ref[lat0 + n_lat:lat0 + n_lat + pad, :] = zeros
    xp_ref[pad:pad + n_ctx, :] = x_ref[0:n_ctx, :]
    xp_ref[lat0:lat0 + n_lat, :] = x_ref[n_ctx:n_tok, :]

    n_tiles = n_tok // tt
    ctx_tiles = n_ctx // tt

    def conv_tile(i, _):
        r0 = pl.multiple_of(i * tt, tt)
        p0 = pl.multiple_of(jnp.where(i < ctx_tiles, r0, r0 + pad), pad)
        ext = xp_ref[pl.ds(p0, tt + 2 * pad), :]
        acc = cb_ref[...] + cw_ref[0:1, :] * ext[pad - 2:pad - 2 + tt, :]
        for j in range(1, CONV_W):
            acc = acc + cw_ref[j:j + 1, :] * ext[pad - 2 + j:pad - 2 + j + tt, :]
        xc_ref[pl.ds(r0, tt), :] = acc
        return 0

    lax.fori_loop(0, n_tiles, conv_tile, 0)

    lam = lam_ref[...]
    sp = jnp.maximum(-lam, 0.0) + jnp.log1p(jnp.exp(-jnp.abs(lam)))
    rate = (-RG_C * LOG2E) * sp

    def coeffs(r0, d):
        xc = xc_ref[pl.ds(r0, tt), :]
        pre = _dot(xc.astype(BF16), w_ref[:, d * 2 * LANES:(d + 1) * 2 * LANES]) \
            + b_ref[:, d * 2 * LANES:(d + 1) * 2 * LANES]
        r = _sigmoid(pre[:, :LANES])
        ig = _sigmoid(pre[:, LANES:])
        a = jnp.exp2(r * rate[d:d + 1, :])
        t = 1.0 - a * a
        root = jnp.where(t > 0.0, t * lax.rsqrt(t), 0.0)
        return a, root * (ig * xc)

    def scan_tile(j, carry):
        cf, cb = carry
        rf = pl.multiple_of(j * tt, tt)
        ib = jnp.where(j < ctx_tiles, ctx_tiles - 1 - j, n_tiles - 1 - (j - ctx_tiles))
        rb = pl.multiple_of(ib * tt, tt)
        af, uf = coeffs(rf, 0)
        ab, ub = coeffs(rb, 1)
        hf, cf = _scan_rows(af, uf, cf, False, sa_ref.at[0], su_ref.at[0])
        hb, cb = _scan_rows(ab, ub, cb, True, sa_ref.at[1], su_ref.at[1])
        h_ref[0, pl.ds(rf, tt), :] = hf
        h_ref[1, pl.ds(rb, tt), :] = hb
        return cf, cb

    zero = jnp.zeros((1, LANES), F32)
    lax.fori_loop(0, n_tiles, scan_tile, (zero, zero))

    def gate_tile(i, _):
        r0 = pl.multiple_of(i * tt, tt)
        tot = h_ref[0, pl.ds(r0, tt), :] + h_ref[1, pl.ds(r0, tt), :]
        o_ref[pl.ds(r0, tt), :] = (tot * jax.nn.gelu(g_ref[pl.ds(r0, tt), :])).astype(o_ref.dtype)
        return 0

    lax.fori_loop(0, n_tiles, gate_tile, 0)


def _rglru(u, conv_w, conv_b, wcat, bcat, lam, batch, n_ctx):
    m = u.shape[0]
    t = m // batch
    ncg = LRU_WIDTH // LANES
    xcol = OFF_LRU // LANES
    return pl.pallas_call(
        functools.partial(_lru_kernel, n_ctx=n_ctx, n_tok=t),
        grid=(batch, ncg),
        in_specs=[
            pl.BlockSpec((t, LANES), lambda b, c: (b, xcol + c)),
            pl.BlockSpec((t, LANES), lambda b, c: (b, xcol + ncg + c)),
            pl.BlockSpec((CONV_W, LANES), lambda b, c: (0, c)),
            pl.BlockSpec((1, LANES), lambda b, c: (0, c)),
            pl.BlockSpec((None, LANES, 4 * LANES), lambda b, c: (c, 0, 0)),
            pl.BlockSpec((None, 1, 4 * LANES), lambda b, c: (c, 0, 0)),
            pl.BlockSpec((2, LANES), lambda b, c: (0, c)),
        ],
        out_specs=pl.BlockSpec((t, LANES), lambda b, c: (b, c)),
        out_shape=jax.ShapeDtypeStruct((m, LRU_WIDTH), BF16),
        scratch_shapes=[pltpu.VMEM((t + 3 * SUBLANES, LANES), F32),
                        pltpu.VMEM((t, LANES), F32),
                        pltpu.VMEM((2, t, LANES), F32),
                        pltpu.VMEM((2, TM, LANES), F32),
                        pltpu.VMEM((2, TM, LANES), F32)],
        compiler_params=_params("parallel", "parallel"),
        name="rglru",
    )(u, u, conv_w, conv_b.reshape(1, -1), wcat, bcat, lam)


def _merge_kernel(ya_ref, yb_ref, yc_ref, ml_ref, xs_ref, g1_ref, wa_ref, wb_ref, wc_ref, wo_ref, o_ref):
    d = D_MODEL
    acc = _sigmoid(ml_ref[:, 0:d]) * _dot(ya_ref[...], wa_ref[...])
    acc = acc + _sigmoid(ml_ref[:, d:2 * d]) * _dot(yb_ref[...], wb_ref[...])
    acc = acc + _sigmoid(ml_ref[:, 2 * d:3 * d]) * _dot(yc_ref[...], wc_ref[...])
    y = _dot(acc.astype(BF16), wo_ref[...])
    o_ref[...] = xs_ref[...] + g1_ref[...] * y


def _merge(ya, yb, yc, u, xs, mods, wa, wb, wc, wo, batch, n_ctx, drop_ctx):
    m, d = xs.shape
    tpb = m // batch // TM
    ctx_tiles = n_ctx // TM
    if drop_ctx:
        out_tpb = tpb - ctx_tiles
        src = lambda i: (i // out_tpb) * tpb + ctx_tiles + i % out_tpb
        row = lambda i: i // out_tpb
        m_out = batch * out_tpb * TM
    else:
        src = lambda i: i
        row = functools.partial(_mod_row, tiles_per_batch=tpb, ctx_tiles=ctx_tiles)
        m_out = m
    br = pl.BlockSpec((TM, HG_WIDTH), lambda i: (src(i), 0))
    full = lambda a: pl.BlockSpec(a.shape, lambda i: (0, 0))
    return pl.pallas_call(
        _merge_kernel,
        grid=(m_out // TM,),
        in_specs=[br, br, br,
                  pl.BlockSpec((TM, 3 * d), lambda i: (src(i), OFF_MERGE)),
                  pl.BlockSpec((TM, d), lambda i: (src(i), 0)),
                  pl.BlockSpec((None, 1, d), lambda i: (row(i), 0, 2)),
                  full(wa), full(wb), full(wc), full(wo)],
        out_specs=pl.BlockSpec((TM, d), lambda i: (i, 0)),
        out_shape=jax.ShapeDtypeStruct((m_out, d), F32),
        compiler_params=_params("parallel"),
        name="merge_out",
    )(ya, yb, yc, u, xs, mods, wa, wb, wc, wo)


def _swiglu_acc(hb, wg_ref, wu_ref, wd_ref):
    acc = jnp.zeros((hb.shape[0], wd_ref.shape[1]), F32)
    for c0 in range(0, D_FF, FF_CHUNK):
        a = _dot(hb, wg_ref[:, c0:c0 + FF_CHUNK])
        b = _dot(hb, wu_ref[:, c0:c0 + FF_CHUNK])
        acc = acc + _dot((_silu(a) * b).astype(BF16), wd_ref[c0:c0 + FF_CHUNK, :])
    return acc


def _final_norm(y, fw):
    ms = jnp.mean(y * y, axis=-1, keepdims=True)
    return y * lax.rsqrt(ms + EPS) * fw


def _ffn_kernel(x_ref, nw_ref, sc_ref, sh_ref, g2_ref, wg_ref, wu_ref, wd_ref, fw_ref, o_ref, *, final):
    x = x_ref[...]
    hb = _norm_mod(x, nw_ref[...], sc_ref[...], sh_ref[...]).astype(BF16)
    y = x + g2_ref[...] * _swiglu_acc(hb, wg_ref, wu_ref, wd_ref)
    o_ref[...] = _final_norm(y, fw_ref[...]) if final else y


def _mod_specs(row, d):
    return [pl.BlockSpec((None, 1, d), lambda i, *_: (row(i), 0, 4)),
            pl.BlockSpec((None, 1, d), lambda i, *_: (row(i), 0, 3)),
            pl.BlockSpec((None, 1, d), lambda i, *_: (row(i), 0, 5))]


def _ffn(xs, nw, mods, wg, wu, wd, fw, tiles_per_batch, ctx_tiles, final):
    m, d = xs.shape
    row = functools.partial(_mod_row, tiles_per_batch=tiles_per_batch, ctx_tiles=ctx_tiles)
    full = lambda a: pl.BlockSpec(a.shape, lambda i: (0, 0))
    vec = pl.BlockSpec((1, d), lambda i: (0, 0))
    return pl.pallas_call(
        functools.partial(_ffn_kernel, final=final),
        grid=(m // TM,),
        in_specs=[pl.BlockSpec((TM, d), lambda i: (i, 0)), vec] + _mod_specs(row, d)
                 + [full(wg), full(wu), full(wd), vec],
        out_specs=pl.BlockSpec((TM, d), lambda i: (i, 0)),
        out_shape=jax.ShapeDtypeStruct((m, d), F32),
        compiler_params=_params("parallel"),
        name="ffn_dense",
    )(xs, nw.reshape(1, d), mods, mods, mods, wg, wu, wd, fw.reshape(1, d))


def _top2(logits):
    lane = lax.broadcasted_iota(jnp.int32, logits.shape, 1)
    m1 = jnp.max(logits, axis=-1, keepdims=True)
    i1 = jnp.min(jnp.where(logits == m1, lane, LANES), axis=-1, keepdims=True)
    rest = jnp.where(lane == i1, -jnp.inf, logits)
    m2 = jnp.max(rest, axis=-1, keepdims=True)
    i2 = jnp.min(jnp.where(rest == m2, lane, LANES), axis=-1, keepdims=True)
    e2 = jnp.exp(m2 - m1)
    w1 = 1.0 / (1.0 + e2)
    return i1, i2, w1, e2 * w1


def _router_kernel(x_ref, nw_ref, sc_ref, sh_ref, rw_ref, meta_ref, wts_ref, cnt_ref, base_ref):
    @pl.when(pl.program_id(0) == 0)
    def _():
        base_ref[...] = jnp.zeros_like(base_ref)

    h = _norm_mod(x_ref[...], nw_ref[...], sc_ref[...], sh_ref[...])
    lane = lax.broadcasted_iota(jnp.int32, (h.shape[0], LANES), 1)
    logits = jnp.full((h.shape[0], LANES), -jnp.inf, F32)
    for e in range(N_EXPERTS):
        logits = jnp.where(lane == e, jnp.sum(h * rw_ref[e:e + 1, :], axis=-1, keepdims=True), logits)
    i1, i2, w1, w2 = _top2(logits)
    hit = jnp.where((lane == i1) | (lane == i2), 1.0, 0.0)
    tm = hit.shape[0]
    tri = lax.broadcasted_iota(jnp.int32, (tm, tm), 0) > lax.broadcasted_iota(jnp.int32, (tm, tm), 1)
    before = _dot(jnp.where(tri, 1.0, 0.0).astype(BF16), hit.astype(BF16)) + base_ref[...]
    r1 = jnp.sum(jnp.where(lane == i1, before, 0.0), axis=-1, keepdims=True).astype(jnp.int32)
    r2 = jnp.sum(jnp.where(lane == i2, before, 0.0), axis=-1, keepdims=True).astype(jnp.int32)
    base_ref[...] += jnp.sum(hit, axis=0, keepdims=True)
    cnt_ref[...] = base_ref[...]
    meta_ref[...] = jnp.where(lane == 0, i1, jnp.where(lane == 1, i2, jnp.where(lane == 2, r1,
                              jnp.where(lane == 3, r2, 0))))
    wts_ref[...] = jnp.where(lane == 0, w1, jnp.where(lane == 1, w2, 0.0))


def _router(xs, nw, mods, rw, row):
    m, d = xs.shape
    vec = pl.BlockSpec((1, d), lambda i: (0, 0))
    tok = pl.BlockSpec((TM, LANES), lambda i: (i, 0))
    return pl.pallas_call(
        _router_kernel,
        grid=(m // TM,),
        in_specs=[pl.BlockSpec((TM, d), lambda i: (i, 0)), vec] + _mod_specs(row, d)[:2]
                 + [pl.BlockSpec((N_EXPERTS, d), lambda i: (0, 0))],
        out_specs=[tok, tok, pl.BlockSpec((1, LANES), lambda i: (0, 0))],
        out_shape=[jax.ShapeDtypeStruct((m, LANES), jnp.int32), jax.ShapeDtypeStruct((m, LANES), F32),
                   jax.ShapeDtypeStruct((1, LANES), F32)],
        scratch_shapes=[pltpu.VMEM((1, LANES), F32)],
        compiler_params=_params("arbitrary"),
        name="moe_router",
    )(xs, nw.reshape(1, d), mods, mods, rw)


def _row_copy(src_ref, s, dst_ref, t, sem):
    return pltpu.make_async_copy(src_ref.at[pl.ds(s, 1)], dst_ref.at[pl.ds(t, 1)], sem)


def _dispatch_kernel(pos_ref, x_ref, nw_ref, sc_ref, sh_ref, xin_ref, xbuf_ref, h_ref, sem):
    del xin_ref
    h_ref[...] = _norm_mod(x_ref[...], nw_ref[...], sc_ref[...], sh_ref[...])
    tm = h_ref.shape[0]

    def issue(r, _):
        _row_copy(h_ref, r, xbuf_ref, pos_ref[0, 2 * r], sem).start(priority=0)
        _row_copy(h_ref, r, xbuf_ref, pos_ref[0, 2 * r + 1], sem).start(priority=1)
        return 0

    lax.fori_loop(0, tm, issue, 0, unroll=DMA_UNROLL)

    def drain(r, _):
        _row_copy(h_ref, 0, xbuf_ref, 0, sem).wait()
        _row_copy(h_ref, 0, xbuf_ref, 0, sem).wait()
        return 0

    lax.fori_loop(0, tm, drain, 0, unroll=DMA_UNROLL)


def _dispatch(xs, nw, mods, pos, row, n_rows):
    m, d = xs.shape
    vec = pl.BlockSpec((1, d), lambda i: (0, 0))
    slab = (d,)
    zeros = jnp.zeros((n_rows,) + slab, F32)
    return pl.pallas_call(
        _dispatch_kernel,
        grid=(m // TM,),
        in_specs=[pl.BlockSpec((None, 1, 2 * TM), lambda i: (i, 0, 0), memory_space=pltpu.SMEM),
                  pl.BlockSpec((TM, d), lambda i: (i, 0)), vec] + _mod_specs(row, d)[:2]
                 + [pl.BlockSpec(memory_space=pl.ANY)],
        out_specs=pl.BlockSpec(memory_space=pl.ANY),
        out_shape=jax.ShapeDtypeStruct((n_rows,) + slab, F32),
        scratch_shapes=[pltpu.VMEM((TM,) + slab, F32), pltpu.SemaphoreType.DMA(())],
        input_output_aliases={5: 0},
        compiler_params=_params("arbitrary"),
        name="moe_dispatch",
    )(pos, xs, nw.reshape(1, d), mods, mods, zeros)


def _expert_ffn_kernel(te_ref, nu_ref, x_ref, wg_ref, wu_ref, wd_ref, y_ref):
    del te_ref
    used = pl.program_id(0) < nu_ref[0]

    @pl.when(used)
    def _():
        y_ref[...] = _swiglu_acc(x_ref[...].astype(BF16), wg_ref, wu_ref, wd_ref)

    @pl.when(jnp.logical_not(used))
    def _():
        y_ref[...] = jnp.zeros_like(y_ref)


def _expert_ffn(xbuf, tile_expert, n_used, wg, wu, wd):
    n_rows = xbuf.shape[0]
    d = wg.shape[1]
    wspec = lambda shape: pl.BlockSpec((None,) + shape, lambda i, te, nu: (te[i], 0, 0))
    tok = pl.BlockSpec((TM, d), lambda i, te, nu: (i, 0))
    return pl.pallas_call(
        _expert_ffn_kernel,
        grid_spec=pltpu.PrefetchScalarGridSpec(
            num_scalar_prefetch=2,
            grid=(n_rows // TM,),
            in_specs=[tok, wspec((d, D_FF)), wspec((d, D_FF)), wspec((D_FF, d))],
            out_specs=tok,
        ),
        out_shape=jax.ShapeDtypeStruct(xbuf.shape, F32),
        compiler_params=_params("arbitrary"),
        name="moe_expert_ffn",
    )(tile_expert, n_used, xbuf, wg, wu, wd)


def _combine_kernel(pos_ref, nxt_ref, x_ref, wts_ref, g2_ref, fw_ref, y_ref, o_ref, buf_ref, sem, *, final):
    i = pl.program_id(0)
    tm = x_ref.shape[0]
    slot = i % 2

    def gather(p_ref, s):
        def issue(r, _):
            _row_copy(y_ref, p_ref[0, 2 * r], buf_ref.at[s, 0], r, sem.at[s]).start(priority=0)
            _row_copy(y_ref, p_ref[0, 2 * r + 1], buf_ref.at[s, 1], r, sem.at[s]).start(priority=1)
            return 0

        lax.fori_loop(0, tm, issue, 0, unroll=DMA_UNROLL)

    @pl.when(i == 0)
    def _():
        gather(pos_ref, 0)

    @pl.when(i + 1 < pl.num_programs(0))
    def _():
        gather(nxt_ref, 1 - slot)

    def drain(r, _):
        _row_copy(y_ref, 0, buf_ref.at[slot, 0], 0, sem.at[slot]).wait()
        _row_copy(y_ref, 0, buf_ref.at[slot, 0], 0, sem.at[slot]).wait()
        return 0

    lax.fori_loop(0, tm, drain, 0, unroll=DMA_UNROLL)
    w = wts_ref[...]
    f = w[:, 0:1] * buf_ref[slot, 0] + w[:, 1:2] * buf_ref[slot, 1]
    y = x_ref[...] + g2_ref[...] * f
    o_ref[...] = _final_norm(y, fw_ref[...]) if final else y


def _combine(xs, wts, mods, fw, ybuf, pos, row, final):
    m, d = xs.shape
    vec = pl.BlockSpec((1, d), lambda i: (0, 0))
    return pl.pallas_call(
        functools.partial(_combine_kernel, final=final),
        grid=(m // TM,),
        in_specs=[pl.BlockSpec((None, 1, 2 * TM), lambda i: (i, 0, 0), memory_space=pltpu.SMEM),
                  pl.BlockSpec((None, 1, 2 * TM), lambda i: (jnp.minimum(i + 1, m // TM - 1), 0, 0),
                               memory_space=pltpu.SMEM),
                  pl.BlockSpec((TM, d), lambda i: (i, 0)),
                  pl.BlockSpec((TM, LANES), lambda i: (i, 0)),
                  _mod_specs(row, d)[2], vec,
                  pl.BlockSpec(memory_space=pl.ANY)],
        out_specs=pl.BlockSpec((TM, d), lambda i: (i, 0)),
        out_shape=jax.ShapeDtypeStruct((m, d), F32),
        scratch_shapes=[pltpu.VMEM((2, 2, TM) + ybuf.shape[1:], F32), pltpu.SemaphoreType.DMA((2,))],
        compiler_params=_params("arbitrary"),
        name="moe_combine",
    )(pos, pos, xs, wts, mods, fw.reshape(1, d), ybuf)


def _moe(xs, nw, mods, rw, wg, wu, wd, fw, tiles_per_batch, ctx_tiles, final):
    m, d = xs.shape
    row = functools.partial(_mod_row, tiles_per_batch=tiles_per_batch, ctx_tiles=ctx_tiles)
    meta, wts, cnt = _router(xs, nw, mods, rw, row)
    counts = cnt[0, :N_EXPERTS].astype(jnp.int32)
    padded = (counts + TM - 1) // TM * TM
    ends = jnp.cumsum(padded)
    starts = ends - padded
    pos = jnp.stack([starts[meta[:, 0]] + meta[:, 2], starts[meta[:, 1]] + meta[:, 3]], axis=-1)
    pos = pos.reshape(m // TM, 1, 2 * TM)
    n_rows = 2 * m + N_EXPERTS * TM
    n_used = (ends[-1] // TM).astype(jnp.int32)
    tile_start = jnp.arange(n_rows // TM, dtype=jnp.int32) * TM
    tile_row = jnp.minimum(tile_start, ends[-1] - 1)
    tile_expert = jnp.sum((tile_row[:, None] >= ends[None, :]).astype(jnp.int32), axis=1)
    xbuf = _dispatch(xs, nw, mods, pos, row, n_rows)
    ybuf = _expert_ffn(xbuf, tile_expert, n_used.reshape(1), wg, wu, wd)
    return _combine(xs, wts, mods, fw, ybuf, pos, row, final)


def _cast_kernel(x_ref, o_ref):
    o_ref[...] = x_ref[...].astype(o_ref.dtype)


def _to_bf16(w, layer):
    cols = w.shape[-1]
    w2 = w.reshape(-1, cols)
    rows = TM * D_FF // cols
    blocks = w2.shape[0] // w.shape[0] // rows
    out = pl.pallas_call(
        _cast_kernel,
        grid=(blocks,),
        in_specs=[pl.BlockSpec((rows, cols), lambda i: (layer * blocks + i, 0))],
        out_specs=pl.BlockSpec((rows, cols), lambda i: (i, 0)),
        out_shape=jax.ShapeDtypeStruct((blocks * rows, cols), BF16),
        compiler_params=_params("parallel"),
        name="weights_to_bf16",
    )(w2)
    return out.reshape(w.shape[1:])


def _column_perm():
    sizes = (512, 512, 512, 512, 512, 512, 128, 128, 512, 512, 3 * D_MODEL)
    starts = np.concatenate([[0], np.cumsum(sizes)[:-1]])
    a_q, a_ff, a_fb, a_v, a_g, b_q, b_k, b_v, c_x, c_g, mrg = [
        np.arange(s, s + n) for s, n in zip(starts, sizes)]
    half = np.concatenate([np.arange(0, HEAD_DIM, 2), np.arange(1, HEAD_DIM, 2)])
    group = ATT_HEADS // ATT_KV_HEADS
    q_cols = []
    for j in range(group):
        q_cols += [b_q[j * HEAD_DIM + half], b_q[(group + j) * HEAD_DIM + half]]
    k_cols = [b_k[h * HEAD_DIM + half] for h in range(ATT_KV_HEADS)]
    return np.concatenate([mrg, a_q, a_ff, a_fb, a_v, a_g, c_x, c_g] + q_cols + k_cols + [b_v])


def _permute_columns(w, perm):
    cuts = np.flatnonzero(np.diff(perm) != 1) + 1
    parts, pending = [], []
    for run in np.split(perm, cuts):
        if len(run) >= LANES:
            if pending:
                parts.append(w[:, np.concatenate(pending)])
                pending = []
            parts.append(w[:, int(run[0]):int(run[-1]) + 1])
        else:
            pending.append(run)
    if pending:
        parts.append(w[:, np.concatenate(pending)])
    return jnp.concatenate(parts, axis=1)


def _att_out_rows():
    group = ATT_HEADS // ATT_KV_HEADS
    idx = []
    for j in range(group):
        idx += [np.arange(j * HEAD_DIM, (j + 1) * HEAD_DIM),
                np.arange((group + j) * HEAD_DIM, (group + j + 1) * HEAD_DIM)]
    return np.concatenate(idx)


def _rope_tables(n_ctx, n_lat):
    rows = n_lat // GRID_W
    row = jnp.repeat(jnp.arange(rows, dtype=F32), GRID_W)
    col = jnp.tile(jnp.arange(GRID_W, dtype=F32), rows)
    pairs = HEAD_DIM // 4
    freqs = ROPE_THETA ** (-jnp.arange(pairs, dtype=F32) / pairs)
    ang = jnp.concatenate([row[:, None] * freqs, col[:, None] * freqs], axis=-1)
    cos, sin = jnp.cos(ang), jnp.sin(ang)
    cos = jnp.concatenate([jnp.ones((n_ctx, HEAD_DIM // 2), F32), cos], axis=0)
    sin = jnp.concatenate([jnp.zeros((n_ctx, HEAD_DIM // 2), F32), sin], axis=0)
    reps = LANES // HEAD_DIM
    return (jnp.tile(jnp.concatenate([cos, cos], axis=-1), (1, reps)),
            jnp.tile(jnp.concatenate([-sin, sin], axis=-1), (1, reps)))


def _lru_weights(wa, wx, ba, bx):
    ncg = LRU_WIDTH // LANES

    def dense(w):
        w = w.reshape(ncg, 2, LRU_BLOCK_W, LRU_BLOCK_W)
        out = jnp.zeros((ncg, LANES, LANES), w.dtype)
        out = out.at[:, :LRU_BLOCK_W, :LRU_BLOCK_W].set(w[:, 0])
        return out.at[:, LRU_BLOCK_W:, LRU_BLOCK_W:].set(w[:, 1])

    wcat = jnp.concatenate([dense(wa[0]), dense(wx[0]), dense(wa[1]), dense(wx[1])], axis=-1).astype(BF16)
    bcat = jnp.concatenate([b.reshape(ncg, 1, LANES) for b in (ba[0], bx[0], ba[1], bx[1])], axis=-1)
    return wcat, bcat


def kernel(x, c, ctx, c_ctx, ada_w, ada_b, mix_norm_w, ffn_norm_w, w_in, hg_lb_logits, hg_norm_w, q_norm_w, k_norm_w, lru_conv_w, lru_conv_b, lru_wa, lru_ba, lru_wx, lru_bx, lru_lambda, w_br_a, w_br_b, w_br_c, w_out, ffn_w_gate, ffn_w_up, ffn_w_down, router_w, moe_w_gate, moe_w_up, moe_w_down, final_norm_w):
    batch, n_lat, d = x.shape
    n_ctx = ctx.shape[1]
    t = n_ctx + n_lat
    depth = ada_w.shape[0]
    assert n_ctx % TM == 0 and n_lat % TM == 0 and batch <= 4

    cvec = jnp.zeros((SUBLANES, d), F32).at[:batch].set(c).at[4].set(c_ctx)
    mods_all = _ada_mods(cvec, ada_w, ada_b).reshape(depth, SUBLANES, 1, 6 * d)

    p = jax.nn.softmax(hg_lb_logits.astype(F32), axis=0)
    cum = jnp.cumsum(p, axis=0)
    lbs = cum - cum[:1]
    log_lb, log_1m_lb = jnp.log(lbs) * LOG2E, jnp.log1p(-lbs) * LOG2E

    cos, sin = _rope_tables(n_ctx, n_lat)
    half = np.concatenate([np.arange(0, HEAD_DIM, 2), np.arange(1, HEAD_DIM, 2)])
    gidx = np.arange(LANES) // HEAD_DIM
    gmat = jnp.asarray(gidx[:, None] == gidx[None, :], BF16)
    col_perm = _column_perm()
    att_rows = _att_out_rows()

    xs = jnp.concatenate([ctx.astype(x.dtype), x], axis=1).reshape(batch * t, d)
    tpb, ctx_tiles = t // TM, n_ctx // TM
    for i in range(depth):
        last = i == depth - 1
        mods = mods_all[i]
        w_i = _permute_columns(w_in[i], col_perm).astype(BF16)
        u = _in_proj(xs, mix_norm_w[i], mods, w_i, tpb, ctx_tiles)

        o_f = _hg_direction(u, log_lb[i, 0], log_1m_lb[i, 0], batch, n_ctx, False)
        y_a = _hg_direction(u, log_lb[i, 1], log_1m_lb[i, 1], batch, n_ctx, True, o_f, hg_norm_w[i])

        reps = LANES // HEAD_DIM
        qw = jnp.tile(q_norm_w[i][half], reps).reshape(1, LANES)
        kw = jnp.tile(k_norm_w[i][half], reps).reshape(1, LANES)
        qr, k0, k1, v0, v1 = _att_prep(u, cos, sin, qw, kw, gmat, batch)
        bound = (1.02 * HEAD_DIM ** 0.5 * float(np.log2(np.e))
                 * jnp.max(jnp.abs(q_norm_w[i])) * jnp.max(jnp.abs(k_norm_w[i]))).reshape(1, 1)
        y_b = _attention(bound, qr, k0, k1, v0, v1, batch, n_ctx)

        wcat, bcat = _lru_weights(lru_wa[i], lru_wx[i], lru_ba[i], lru_bx[i])
        y_c = _rglru(u, lru_conv_w[i], lru_conv_b[i], wcat, bcat, lru_lambda[i], batch, n_ctx)

        xs = _merge(y_a, y_b, y_c, u, xs, mods, w_br_a[i].astype(BF16), w_br_b[i][att_rows].astype(BF16),
                    w_br_c[i].astype(BF16), w_out[i].astype(BF16), batch, n_ctx, last)
        if last:
            tpb, ctx_tiles = n_lat // TM, 0
        if i % 2 == 0:
            j = i // 2
            xs = _ffn(xs, ffn_norm_w[i], mods, _to_bf16(ffn_w_gate, j), _to_bf16(ffn_w_up, j),
                      _to_bf16(ffn_w_down, j), final_norm_w, tpb, ctx_tiles, last)
        else:
            j = i // 2
            rw = router_w[j].T
            xs = _moe(xs, ffn_norm_w[i], mods, rw, _to_bf16(moe_w_gate, j), _to_bf16(moe_w_up, j),
                      _to_bf16(moe_w_down, j), final_norm_w, tpb, ctx_tiles, last)
    return xs.reshape(batch, n_lat, d)
```
